```python
import jax, jax.numpy as jnp
from jax import lax
import numpy as np

D_MODEL = 2048
BATCH = 16
SEQ = 2048
DEPTH = 1
DEC_BATCH = 16
DEC_SEQ = 64
PAST_LEN = 1024

CHUNK = 64
HEAD_DIM = 64
N_HEADS = 16
N_KV_HEADS = 4
GROUP = N_HEADS // N_KV_HEADS
ATTN_WIDTH = N_HEADS * HEAD_DIM
KV_WIDTH = N_KV_HEADS * HEAD_DIM
WINDOW = 128
WINDOW_CHUNKS = WINDOW // CHUNK
D_CONV = 1024
CONV_K = 31
D_FF = 5632
FFN_CONV_K = 3
ROPE_THETA = 10000.0
EPS = 1e-6
NEG_INF = -1e30
N_IN = 2 * D_CONV + ATTN_WIDTH + 2 * KV_WIDTH + 2 * D_MODEL

kernel_name = "hybrid_streaming_conformer_swa_sink_step"


def _rms_norm(x, g):
    xf = x.astype(jnp.float32)
    y = xf * lax.rsqrt(jnp.mean(xf * xf, axis=-1, keepdims=True) + EPS)
    return (y * g.astype(jnp.float32)).astype(x.dtype)


def _layer_norm(x, g, b):
    xf = x.astype(jnp.float32)
    mu = jnp.mean(xf, axis=-1, keepdims=True)
    var = jnp.mean(jnp.square(xf - mu), axis=-1, keepdims=True)
    y = (xf - mu) * lax.rsqrt(var + EPS)
    return (y * g.astype(jnp.float32) + b.astype(jnp.float32)).astype(x.dtype)


def _causal_dwconv(x, ctx, w, b):
    xp = jnp.concatenate([ctx.astype(x.dtype), x], axis=1)
    y = lax.conv_general_dilated(xp, w.astype(x.dtype)[:, None, :], window_strides=(1,), padding='VALID',
                                 dimension_numbers=('NWC', 'WIO', 'NWC'), feature_group_count=x.shape[-1])
    k = w.shape[0]
    return y + b.astype(x.dtype), xp[:, xp.shape[1] - (k - 1):]


def _rope(x, pos):
    half = HEAD_DIM // 2
    inv_freq = 1.0 / (ROPE_THETA ** (jnp.arange(half, dtype=jnp.float32) / half))
    ang = pos.astype(jnp.float32)[:, None] * inv_freq[None, :]
    cos = jnp.cos(ang)[None, :, None, :]
    sin = jnp.sin(ang)[None, :, None, :]
    xf = x.astype(jnp.float32)
    x1, x2 = xf[..., :half], xf[..., half:]
    return jnp.concatenate([x1 * cos - x2 * sin, x2 * cos + x1 * sin], axis=-1).astype(x.dtype)


def _sink_attention(q, k, v, mask, sinks):
    s = jnp.einsum('bnqhgd,bnkhd->bnhgqk', q, k, preferred_element_type=jnp.float32) * (HEAD_DIM ** -0.5)
    s = jnp.where(mask[None, :, None, None], s, NEG_INF)
    sink = sinks.astype(jnp.float32).reshape(1, 1, N_KV_HEADS, GROUP, 1, 1)
    m = jnp.maximum(jnp.max(s, axis=-1, keepdims=True), sink)
    p = jnp.exp(s - m)
    denom = jnp.sum(p, axis=-1, keepdims=True) + jnp.exp(sink - m)
    return jnp.einsum('bnhgqk,bnkhd->bnqhgd', (p / denom).astype(v.dtype), v)


def _attend_prompt(q, k, v, sinks, win_rows):
    B, T = q.shape[0], q.shape[1]
    nc = T // CHUNK
    pad = WINDOW_CHUNKS * CHUNK
    qb = q.reshape(B, nc, CHUNK, N_KV_HEADS, GROUP, HEAD_DIM)
    zeros = jnp.zeros((B, pad, N_KV_HEADS, HEAD_DIM), k.dtype)
    kp = jnp.concatenate([zeros, k], axis=1).reshape(B, nc + WINDOW_CHUNKS, CHUNK, N_KV_HEADS, HEAD_DIM)
    vp = jnp.concatenate([zeros.astype(v.dtype), v], axis=1).reshape(B, nc + WINDOW_CHUNKS, CHUNK, N_KV_HEADS, HEAD_DIM)
    kb = jnp.concatenate([kp[:, j:j + nc] for j in range(WINDOW_CHUNKS + 1)], axis=2)
    vb = jnp.concatenate([vp[:, j:j + nc] for j in range(WINDOW_CHUNKS + 1)], axis=2)
    key_pos = jnp.arange(nc)[:, None] * CHUNK - pad + jnp.arange((WINDOW_CHUNKS + 1) * CHUNK)[None, :]
    mask = (key_pos >= 0)[:, None, :]
    o = _sink_attention(qb, kb, vb, mask, sinks).reshape(B, T, ATTN_WIDTH)
    return o, k[:, T - win_rows:], v[:, T - win_rows:]


def _attend_sample(q, k, v, k_cache, v_cache, sinks):
    B, T = q.shape[0], q.shape[1]
    w = k_cache.shape[1]
    kk = jnp.concatenate([k_cache.astype(k.dtype), k], axis=1)
    vv = jnp.concatenate([v_cache.astype(v.dtype), v], axis=1)
    qb = q.reshape(B, 1, T, N_KV_HEADS, GROUP, HEAD_DIM)
    mask = jnp.ones((1, 1, w + T), dtype=bool)
    o = _sink_attention(qb, kk[:, None], vv[:, None], mask, sinks).reshape(B, T, ATTN_WIDTH)
    return o, kk[:, T:], vv[:, T:]


def _layer(x, c, pos, conv_ctx, ffn_ctx, k_cache, v_cache, win_rows,
           mod_w, mod_b, norm1_g, w_in, b_in, conv_w, conv_b, ln_g, ln_b, conv_out_w,
           q_norm_g, k_norm_g, sinks, attn_o_w, w_out, norm2_g, ffn_up_w, ffn_conv_w, ffn_conv_b, ffn_down_w):
    B, T, _ = x.shape
    mod = jax.nn.silu(c) @ mod_w + mod_b
    sh1, sc1, g1, sh2, sc2, g2 = jnp.split(mod[:, None, :], 6, axis=-1)
    h = _rms_norm(x, norm1_g) * (1 + sc1) + sh1
    z = h @ w_in + b_in
    o1 = D_CONV
    o2 = o1 + D_CONV
    o3 = o2 + ATTN_WIDTH
    o4 = o3 + KV_WIDTH
    o5 = o4 + KV_WIDTH
    o6 = o5 + D_MODEL
    za, zb, zq, zk, zv, zgc, zga = jnp.split(z, [o1, o2, o3, o4, o5, o6], axis=-1)
    glu = za * jax.nn.sigmoid(zb)
    dw, new_conv = _causal_dwconv(glu, conv_ctx, conv_w, conv_b)
    y_conv = jax.nn.silu(_layer_norm(dw, ln_g, ln_b)) @ conv_out_w
    q = _rope(_rms_norm(zq.reshape(B, T, N_HEADS, HEAD_DIM), q_norm_g), pos)
    k = _rope(_rms_norm(zk.reshape(B, T, N_KV_HEADS, HEAD_DIM), k_norm_g), pos)
    v = zv.reshape(B, T, N_KV_HEADS, HEAD_DIM)
    if k_cache is None:
        o, new_k, new_v = _attend_prompt(q, k, v, sinks, win_rows)
    else:
        o, new_k, new_v = _attend_sample(q, k, v, k_cache, v_cache, sinks)
    y_attn = o @ attn_o_w
    merged = jax.nn.sigmoid(zgc) * y_conv + jax.nn.sigmoid(zga) * y_attn
    x = x + g1 * (merged @ w_out)
    h2 = _rms_norm(x, norm2_g) * (1 + sc2) + sh2
    up = h2 @ ffn_up_w
    ug, uv = jnp.split(up, 2, axis=-1)
    ugc, new_ffn = _causal_dwconv(ug, ffn_ctx, ffn_conv_w, ffn_conv_b)
    x = x + g2 * ((jax.nn.silu(ugc) * uv) @ ffn_down_w)
    return x, new_conv, new_k, new_v, new_ffn


def setup_inputs(seed: int = 0) -> dict:
    key = jax.random.key(seed)
    ks = jax.random.split(key, 32)
    f32 = jnp.float32
    win_rows = min(WINDOW, PAST_LEN)

    def nrm(k, shape, scale):
        return jax.random.normal(k, shape, f32) * scale

    return {
        "x_prompt": nrm(ks[0], (BATCH, SEQ, D_MODEL), 1.0),
        "x_sample": nrm(ks[1], (DEC_BATCH, DEC_SEQ, D_MODEL), 1.0),
        "c_prompt": nrm(ks[2], (BATCH, D_MODEL), 1.0),
        "c_sample": nrm(ks[3], (DEC_BATCH, D_MODEL), 1.0),
        "cache_conv": nrm(ks[4], (DEPTH, DEC_BATCH, CONV_K - 1, D_CONV), 1.0),
        "cache_k": nrm(ks[5], (DEPTH, DEC_BATCH, win_rows, N_KV_HEADS, HEAD_DIM), 1.0),
        "cache_v": nrm(ks[6], (DEPTH, DEC_BATCH, win_rows, N_KV_HEADS, HEAD_DIM), 1.0),
        "cache_ffn_conv": nrm(ks[7], (DEPTH, DEC_BATCH, FFN_CONV_K - 1, D_FF), 1.0),
        "mod_w": nrm(ks[8], (DEPTH, D_MODEL, 6 * D_MODEL), 0.2 * D_MODEL ** -0.5),
        "mod_b": nrm(ks[9], (DEPTH, 6 * D_MODEL), 0.01),
        "norm1_g": 1.0 + nrm(ks[10], (DEPTH, D_MODEL), 0.02),
        "w_in": nrm(ks[11], (DEPTH, D_MODEL, N_IN), D_MODEL ** -0.5),
        "b_in": nrm(ks[12], (DEPTH, N_IN), 0.01),
        "conv_w": nrm(ks[13], (DEPTH, CONV_K, D_CONV), CONV_K ** -0.5),
        "conv_b": nrm(ks[14], (DEPTH, D_CONV), 0.01),
        "ln_g": 1.0 + nrm(ks[15], (DEPTH, D_CONV), 0.02),
        "ln_b": nrm(ks[16], (DEPTH, D_CONV), 0.01),
        "conv_out_w": nrm(ks[17], (DEPTH, D_CONV, D_MODEL), D_CONV ** -0.5),
        "q_norm_g": 1.0 + nrm(ks[18], (DEPTH, HEAD_DIM), 0.02),
        "k_norm_g": 1.0 + nrm(ks[19], (DEPTH, HEAD_DIM), 0.02),
        "sinks": nrm(ks[20], (DEPTH, N_HEADS), 0.5),
        "attn_o_w": nrm(ks[21], (DEPTH, ATTN_WIDTH, D_MODEL), ATTN_WIDTH ** -0.5),
        "w_out": nrm(ks[22], (DEPTH, D_MODEL, D_MODEL), D_MODEL ** -0.5),
        "norm2_g": 1.0 + nrm(ks[23], (DEPTH, D_MODEL), 0.02),
        "ffn_up_w": nrm(ks[24], (DEPTH, D_MODEL, 2 * D_FF), D_MODEL ** -0.5),
        "ffn_conv_w": nrm(ks[25], (DEPTH, FFN_CONV_K, D_FF), FFN_CONV_K ** -0.5),
        "ffn_conv_b": nrm(ks[26], (DEPTH, D_FF), 0.01),
        "ffn_down_w": nrm(ks[27], (DEPTH, D_FF, D_MODEL), D_FF ** -0.5),
    }


def reference(x_prompt, x_sample, c_prompt, c_sample, cache_conv, cache_k, cache_v, cache_ffn_conv,
              mod_w, mod_b, norm1_g, w_in, b_in, conv_w, conv_b, ln_g, ln_b, conv_out_w,
              q_norm_g, k_norm_g, sinks, attn_o_w, w_out, norm2_g, ffn_up_w, ffn_conv_w, ffn_conv_b, ffn_down_w):
    win_rows = cache_k.shape[2]
    pos_p = jnp.arange(x_prompt.shape[1])
    pos_s = PAST_LEN + jnp.arange(x_sample.shape[1])
    bp = x_prompt.shape[0]
    yp, ys = x_prompt, x_sample
    conv_p, conv_s, k_p, k_s, v_p, v_s, ffn_p, ffn_s = [], [], [], [], [], [], [], []
    for l in range(DEPTH):
        w = (mod_w[l], mod_b[l], norm1_g[l], w_in[l], b_in[l], conv_w[l], conv_b[l], ln_g[l], ln_b[l],
             conv_out_w[l], q_norm_g[l], k_norm_g[l], sinks[l], attn_o_w[l], w_out[l], norm2_g[l],
             ffn_up_w[l], ffn_conv_w[l], ffn_conv_b[l], ffn_down_w[l])
        zc = jnp.zeros((bp, CONV_K - 1, D_CONV), yp.dtype)
        zf = jnp.zeros((bp, FFN_CONV_K - 1, D_FF), yp.dtype)
        yp, nc_p, nk_p, nv_p, nf_p = _layer(yp, c_prompt, pos_p, zc, zf, None, None, win_rows, *w)
        ys, nc_s, nk_s, nv_s, nf_s = _layer(ys, c_sample, pos_s, cache_conv[l], cache_ffn_conv[l],
                                            cache_k[l], cache_v[l], win_rows, *w)
        conv_p.append(nc_p); conv_s.append(nc_s)
        k_p.append(nk_p); k_s.append(nk_s)
        v_p.append(nv_p); v_s.append(nv_s)
        ffn_p.append(nf_p); ffn_s.append(nf_s)
    return (yp, ys, jnp.stack(conv_p), jnp.stack(conv_s), jnp.stack(k_p), jnp.stack(k_s),
            jnp.stack(v_p), jnp.stack(v_s), jnp.stack(ffn_p), jnp.stack(ffn_s))
```

```python
import functools

import jax
import jax.numpy as jnp
from jax import lax
from jax.experimental import pallas as pl
from jax.experimental.pallas import tpu as pltpu

CHUNK = 64
HEAD_DIM = 64
WINDOW = 128
PAST_LEN = 1024
ROPE_THETA = 10000.0
EPS = 1e-6
NEG_INF = -1e30
LANES = 128
SUBLANES = 8
VMEM_LIMIT_BYTES = 56 * 1024 * 1024

F32 = jnp.float32
BF16 = jnp.bfloat16


def _params(n_axes):
    return pltpu.CompilerParams(dimension_semantics=("arbitrary",) * n_axes,
                                vmem_limit_bytes=VMEM_LIMIT_BYTES)


def _sigmoid(x):
    return 1.0 / (1.0 + jnp.exp(-x))


def _mod_kernel(c_ref, w_ref, b_ref, o_ref):
    c = c_ref[...]
    a = (c * _sigmoid(c)).astype(BF16)
    o_ref[...] = jnp.dot(a, w_ref[...].astype(BF16), preferred_element_type=F32) + b_ref[...]


def _mod(c_all, mod_w, mod_b, tn=1024):
    m, d = c_all.shape
    n = mod_w.shape[1]
    return pl.pallas_call(
        _mod_kernel,
        grid=(n // tn,),
        in_specs=[pl.BlockSpec((m, d), lambda j: (0, 0)),
                  pl.BlockSpec((d, tn), lambda j: (0, j)),
                  pl.BlockSpec((1, tn), lambda j: (0, j))],
        out_specs=pl.BlockSpec((m, tn), lambda j: (0, j)),
        out_shape=jax.ShapeDtypeStruct((m, n), F32),
        compiler_params=_params(1),
        name="mod",
    )(c_all, mod_w, mod_b)


def _adaln(x, g, sc, sh):
    ms = jnp.mean(x * x, axis=-1, keepdims=True)
    y = x * lax.rsqrt(ms + EPS) * g
    return y * (1.0 + sc) + sh


def _in_kernel(x_ref, sc_ref, sh_ref, g_ref, w_ref, b_ref, o_ref, h_ref):
    nb, tt, d = x_ref.shape

    @pl.when(pl.program_id(2) == 0)
    def _():
        h = _adaln(x_ref[...], g_ref[...], sc_ref[...], sh_ref[...])
        h_ref[...] = h.reshape(nb * tt, d).astype(BF16)

    z = jnp.dot(h_ref[...], w_ref[...], preferred_element_type=F32) + b_ref[...]
    o_ref[...] = z.reshape(nb, tt, -1).astype(o_ref.dtype)


def _in_proj(x, mod3, row0, norm_g, w, b, nb, tt, tn):
    bsz, t, d = x.shape
    n = w.shape[1]
    return pl.pallas_call(
        _in_kernel,
        grid=(bsz // nb, t // tt, n // tn),
        in_specs=[pl.BlockSpec((nb, tt, d), lambda i, s, j: (i, s, 0)),
                  pl.BlockSpec((nb, 1, d), lambda i, s, j: (row0 // nb + i, 0, 1)),
                  pl.BlockSpec((nb, 1, d), lambda i, s, j: (row0 // nb + i, 0, 0)),
                  pl.BlockSpec((1, d), lambda i, s, j: (0, 0)),
                  pl.BlockSpec((d, tn), lambda i, s, j: (0, j)),
                  pl.BlockSpec((1, tn), lambda i, s, j: (0, j))],
        out_specs=pl.BlockSpec((nb, tt, tn), lambda i, s, j: (i, s, j)),
        out_shape=jax.ShapeDtypeStruct((bsz, t, n), BF16),
        scratch_shapes=[pltpu.VMEM((nb * tt, d), BF16)],
        compiler_params=_params(3),
        name="in_proj",
    )(x, mod3, mod3, norm_g, w, b)


CONV_PAD = 32
CONV_STEPS = 16


def _slab_rows(rows):
    pitch = -(-rows // SUBLANES) * SUBLANES
    return pitch if (pitch // SUBLANES) % 2 else pitch + SUBLANES


def _conv_kernel(za_ref, zb_ref, ctx_ref, w_ref, b_ref, lg_ref, lb_ref, a_ref, nc_ref, gbuf, dwbuf):
    nb, tt, c = za_ref.shape
    nct = c // LANES
    kw = w_ref.shape[0]
    tp = gbuf.shape[1] // nct
    tp2 = dwbuf.shape[1] // nct
    lead = CONV_PAD - (kw - 1)
    t = pl.program_id(1)

    @pl.when(t == 0)
    def _():
        for j in range(nct):
            gbuf[:, j * tp:j * tp + CONV_PAD, :] = ctx_ref[:, :, j * LANES:(j + 1) * LANES]

    @pl.when(t > 0)
    def _():
        for j in range(nct):
            gbuf[:, j * tp:j * tp + CONV_PAD, :] = gbuf[:, j * tp + tt:j * tp + tt + CONV_PAD, :]

    for j in range(nct):
        ls = slice(j * LANES, (j + 1) * LANES)
        glu = za_ref[:, :, ls].astype(F32) * _sigmoid(zb_ref[:, :, ls].astype(F32))
        gbuf[:, j * tp + CONV_PAD:j * tp + CONV_PAD + tt, :] = glu
        nc_ref[:, :, ls] = gbuf[:, j * tp + tt:j * tp + tt + CONV_PAD, :]

    w = [w_ref[k] for k in range(kw)]
    bias = b_ref[...]
    nblk = tt // CONV_STEPS

    def body(i, carry):
        n = i // nblk
        t0 = (i % nblk) * CONV_STEPS
        acc = [bias] * CONV_STEPS
        for m in range(CONV_STEPS + kw - 1):
            g = gbuf[n, pl.ds(t0 + lead + m, SUBLANES, stride=tp), :]
            for s in range(max(0, m - (kw - 1)), min(CONV_STEPS - 1, m) + 1):
                acc[s] = acc[s] + g * w[m - s]
        for s in range(CONV_STEPS):
            dwbuf[n, pl.ds(t0 + s, SUBLANES, stride=tp2), :] = acc[s]
        return carry

    lax.fori_loop(0, nb * nblk, body, 0)

    slabs = [dwbuf[:, j * tp2:j * tp2 + tt, :] for j in range(nct)]
    mu = jnp.sum(functools.reduce(jnp.add, slabs), axis=-1, keepdims=True) * (1.0 / c)
    cen = [d - mu for d in slabs]
    var = jnp.sum(functools.reduce(jnp.add, [x * x for x in cen]), axis=-1, keepdims=True) * (1.0 / c)
    inv = lax.rsqrt(var + EPS)
    for j in range(nct):
        ls = slice(j * LANES, (j + 1) * LANES)
        y = cen[j] * inv * lg_ref[:, ls] + lb_ref[:, ls]
        a_ref[:, :, ls] = (y * _sigmoid(y)).astype(a_ref.dtype)


def _conv_branch(z, ctx_pad, conv_w, conv_b, ln_g, ln_b, nb, tt):
    bsz, t, _ = z.shape
    kw, c = conv_w.shape
    nct = c // LANES
    assert nct == SUBLANES and tt % CONV_STEPS == 0
    tp = _slab_rows(CONV_PAD + tt)
    tp2 = _slab_rows(tt)
    return pl.pallas_call(
        _conv_kernel,
        grid=(bsz // nb, t // tt),
        in_specs=[pl.BlockSpec((nb, tt, c), lambda i, s: (i, s, 0)),
                  pl.BlockSpec((nb, tt, c), lambda i, s: (i, s, 1)),
                  pl.BlockSpec((nb, CONV_PAD, c), lambda i, s: (i, 0, 0)),
                  pl.BlockSpec((kw, nct, LANES), lambda i, s: (0, 0, 0)),
                  pl.BlockSpec((nct, LANES), lambda i, s: (0, 0)),
                  pl.BlockSpec((1, c), lambda i, s: (0, 0)),
                  pl.BlockSpec((1, c), lambda i, s: (0, 0))],
        out_specs=[pl.BlockSpec((nb, tt, c), lambda i, s: (i, s, 0)),
                   pl.BlockSpec((nb, CONV_PAD, c), lambda i, s: (i, 0, 0))],
        out_shape=[jax.ShapeDtypeStruct((bsz, t, c), BF16),
                   jax.ShapeDtypeStruct((bsz, CONV_PAD, c), F32)],
        scratch_shapes=[pltpu.VMEM((nb, nct * tp, LANES), F32),
                        pltpu.VMEM((nb, nct * tp2, LANES), F32)],
        compiler_params=_params(2),
        name="conv_branch",
    )(z, z, ctx_pad, conv_w.reshape(kw, nct, LANES), conv_b.reshape(nct, LANES), ln_g, ln_b)


def _pair_norm_rope(x, g, cos, sin):
    lane = lax.broadcasted_iota(jnp.int32, x.shape, 1)
    first = lane < HEAD_DIM
    sq = x * x
    s_a = jnp.sum(jnp.where(first, sq, 0.0), axis=-1, keepdims=True)
    s_b = jnp.sum(jnp.where(first, 0.0, sq), axis=-1, keepdims=True)
    ms = jnp.where(first, s_a, s_b) * (1.0 / HEAD_DIM)
    y = x * lax.rsqrt(ms + EPS) * g
    lower = (lane & (HEAD_DIM - 1)) < (HEAD_DIM // 2)
    partner = jnp.where(lower, pltpu.roll(y, LANES - HEAD_DIM // 2, 1), pltpu.roll(y, HEAD_DIM // 2, 1))
    return y * cos + partner * sin


def _attn_kernel(sink_ref, q_ref, k_ref, v_ref, kc_ref, vc_ref, cos_ref, sin_ref, qg_ref, kg_ref,
                 o_ref, nk_ref, nv_ref, kf, vf, kx, vx, qs, *, mask_prefix):
    tq = q_ref.shape[1]
    kvw = k_ref.shape[2]
    n_kv = kvw // HEAD_DIM
    t = pl.program_id(1)
    cos = cos_ref[...]
    sin = sin_ref[...]

    @pl.when(t == 0)
    def _():
        kf[0:WINDOW, :] = kc_ref[0]
        vf[0:WINDOW, :] = vc_ref[0]

    @pl.when(t > 0)
    def _():
        kf[0:WINDOW, :] = kf[tq:tq + WINDOW, :]
        vf[0:WINDOW, :] = vf[tq:tq + WINDOW, :]

    k_new = k_ref[0].astype(F32)
    for j in range(kvw // LANES):
        ls = slice(j * LANES, (j + 1) * LANES)
        kf[WINDOW:WINDOW + tq, ls] = _pair_norm_rope(k_new[:, ls], kg_ref[...], cos, sin)
    vf[WINDOW:WINDOW + tq, :] = v_ref[0].astype(F32)
    nk_ref[0] = kf[tq:tq + WINDOW, :]
    nv_ref[0] = vf[tq:tq + WINDOW, :]

    lane = lax.broadcasted_iota(jnp.int32, (WINDOW + tq, LANES), 1)
    first = lane < HEAD_DIM
    for src, dst in ((kf, kx), (vf, vx)):
        for j in range(kvw // LANES):
            tile = src[:, j * LANES:(j + 1) * LANES]
            swapped = pltpu.roll(tile, HEAD_DIM, 1)
            variants = (jnp.where(first, tile, 0.0), jnp.where(first, 0.0, swapped),
                        jnp.where(first, swapped, 0.0), jnp.where(first, 0.0, tile))
            for m, val in enumerate(variants):
                dst[:, (4 * j + m) * LANES:(4 * j + m + 1) * LANES] = val.astype(BF16)

    scale = HEAD_DIM ** -0.5
    for j in range(q_ref.shape[2] // LANES):
        ls = slice(j * LANES, (j + 1) * LANES)
        qn = _pair_norm_rope(q_ref[0, :, ls].astype(F32), qg_ref[...], cos, sin)
        qs[:, ls] = (qn * scale).astype(BF16)

    n_keys = WINDOW + CHUNK
    group = q_ref.shape[2] // kvw
    rows = group * CHUNK
    row = lax.broadcasted_iota(jnp.int32, (rows, 1), 0)
    col = lax.broadcasted_iota(jnp.int32, (rows, n_keys), 1)
    dn = (((1,), (1,)), ((), ()))

    def chunk_body(c, carry):
        r0 = pl.multiple_of(c * CHUNK, CHUNK)
        for h in range(n_kv):
            q2 = jnp.concatenate([qs[pl.ds(r0, CHUNK), (2 * h) * LANES:(2 * h + 1) * LANES],
                                  qs[pl.ds(r0, CHUNK), (2 * h + 1) * LANES:(2 * h + 2) * LANES]], axis=0)
            k_lo = kx[pl.ds(r0, n_keys), (2 * h) * LANES:(2 * h + 1) * LANES]
            k_hi = kx[pl.ds(r0, n_keys), (2 * h + 1) * LANES:(2 * h + 2) * LANES]
            s = jnp.concatenate([lax.dot_general(q2, k_lo, dn, preferred_element_type=F32),
                                 lax.dot_general(q2, k_hi, dn, preferred_element_type=F32)], axis=0)
            sink = jnp.where(row < CHUNK, sink_ref[4 * h],
                             jnp.where(row < 2 * CHUNK, sink_ref[4 * h + 2],
                                       jnp.where(row < 3 * CHUNK, sink_ref[4 * h + 1], sink_ref[4 * h + 3])))
            if mask_prefix:
                first_key = WINDOW - (t * tq + r0)
                s = jnp.where(col >= first_key, s, NEG_INF)
            m = jnp.maximum(jnp.max(s, axis=-1, keepdims=True), sink)
            p = jnp.exp(s - m)
            denom = jnp.sum(p, axis=-1, keepdims=True) + jnp.exp(sink - m)
            p = (p / denom).astype(BF16)
            v_lo = vx[pl.ds(r0, n_keys), (2 * h) * LANES:(2 * h + 1) * LANES]
            v_hi = vx[pl.ds(r0, n_keys), (2 * h + 1) * LANES:(2 * h + 2) * LANES]
            o2 = (jnp.dot(p[0:2 * CHUNK], v_lo, preferred_element_type=F32)
                  + jnp.dot(p[2 * CHUNK:4 * CHUNK], v_hi, preferred_element_type=F32))
            o_ref[0, pl.ds(r0, CHUNK), (2 * h) * LANES:(2 * h + 1) * LANES] = o2[0:CHUNK].astype(o_ref.dtype)
            o_ref[0, pl.ds(r0, CHUNK), (2 * h + 1) * LANES:(2 * h + 2) * LANES] = o2[CHUNK:2 * CHUNK].astype(o_ref.dtype)
        return carry

    lax.fori_loop(0, tq // CHUNK, chunk_body, 0)


def _attention(z, k_cache, v_cache, cos, sin, q_g, k_g, sinks, q_col, aw, kvw, tq, mask_prefix):
    bsz, t, _ = z.shape
    assert q_col % aw == 0 and (q_col + aw) % kvw == 0 and aw // kvw == 4 and kvw % LANES == 0
    k_blk = (q_col + aw) // kvw
    return pl.pallas_call(
        functools.partial(_attn_kernel, mask_prefix=mask_prefix),
        grid=(bsz, t // tq),
        in_specs=[pl.BlockSpec(memory_space=pltpu.SMEM),
                  pl.BlockSpec((1, tq, aw), lambda i, s: (i, s, q_col // aw)),
                  pl.BlockSpec((1, tq, kvw), lambda i, s: (i, s, k_blk)),
                  pl.BlockSpec((1, tq, kvw), lambda i, s: (i, s, k_blk + 1)),
                  pl.BlockSpec((1, WINDOW, kvw), lambda i, s: (i, 0, 0)),
                  pl.BlockSpec((1, WINDOW, kvw), lambda i, s: (i, 0, 0)),
                  pl.BlockSpec((tq, LANES), lambda i, s: (s, 0)),
                  pl.BlockSpec((tq, LANES), lambda i, s: (s, 0)),
                  pl.BlockSpec((1, LANES), lambda i, s: (0, 0)),
                  pl.BlockSpec((1, LANES), lambda i, s: (0, 0))],
        out_specs=[pl.BlockSpec((1, tq, aw), lambda i, s: (i, s, 0)),
                   pl.BlockSpec((1, WINDOW, kvw), lambda i, s: (i, 0, 0)),
                   pl.BlockSpec((1, WINDOW, kvw), lambda i, s: (i, 0, 0))],
        scratch_shapes=[pltpu.VMEM((WINDOW + tq, kvw), F32),
                        pltpu.VMEM((WINDOW + tq, kvw), F32),
                        pltpu.VMEM((WINDOW + tq, 4 * kvw), BF16),
                        pltpu.VMEM((WINDOW + tq, 4 * kvw), BF16),
                        pltpu.VMEM((tq, aw), BF16)],
        out_shape=[jax.ShapeDtypeStruct((bsz, t, aw), BF16),
                   jax.ShapeDtypeStruct((bsz, WINDOW, kvw), F32),
                   jax.ShapeDtypeStruct((bsz, WINDOW, kvw), F32)],
        compiler_params=_params(2),
        name="attention",
    )(sinks, z, z, z, k_cache, v_cache, cos, sin, q_g, k_g)


def _merge_kernel(a_ref, o_ref, gc_ref, ga_ref, wc_ref, wa_ref, wo_ref, x_ref, g1_ref, y_ref, mg):
    nb, tt, d = x_ref.shape
    tn = wc_ref.shape[1]
    n = pl.program_id(2)
    a = a_ref[...].reshape(nb * tt, -1)
    o = o_ref[...].reshape(nb * tt, -1)
    yc = jnp.dot(a, wc_ref[...], preferred_element_type=F32)
    ya = jnp.dot(o, wa_ref[...], preferred_element_type=F32)
    gc = gc_ref[...].astype(F32).reshape(nb * tt, tn)
    ga = ga_ref[...].astype(F32).reshape(nb * tt, tn)
    merged = _sigmoid(gc) * yc + _sigmoid(ga) * ya
    mg[:, pl.ds(pl.multiple_of(n * tn, tn), tn)] = merged.astype(BF16)

    @pl.when(n == pl.num_programs(2) - 1)
    def _():
        proj = jnp.dot(mg[...], wo_ref[...], preferred_element_type=F32).reshape(nb, tt, d)
        y_ref[...] = x_ref[...] + g1_ref[...] * proj


def _merge(a, o, z, gc_col, ga_col, wc, wa, wo, x, mod3, row0, nb, tt, tn):
    bsz, t, d = x.shape
    c = a.shape[2]
    aw = o.shape[2]
    assert gc_col % tn == 0 and ga_col % tn == 0 and d % tn == 0
    return pl.pallas_call(
        _merge_kernel,
        grid=(bsz // nb, t // tt, d // tn),
        in_specs=[pl.BlockSpec((nb, tt, c), lambda i, s, n: (i, s, 0)),
                  pl.BlockSpec((nb, tt, aw), lambda i, s, n: (i, s, 0)),
                  pl.BlockSpec((nb, tt, tn), lambda i, s, n: (i, s, gc_col // tn + n)),
                  pl.BlockSpec((nb, tt, tn), lambda i, s, n: (i, s, ga_col // tn + n)),
                  pl.BlockSpec((c, tn), lambda i, s, n: (0, n)),
                  pl.BlockSpec((aw, tn), lambda i, s, n: (0, n)),
                  pl.BlockSpec((d, d), lambda i, s, n: (0, 0)),
                  pl.BlockSpec((nb, tt, d), lambda i, s, n: (i, s, 0)),
                  pl.BlockSpec((nb, 1, d), lambda i, s, n: (row0 // nb + i, 0, 2))],
        out_specs=pl.BlockSpec((nb, tt, d), lambda i, s, n: (i, s, 0)),
        out_shape=jax.ShapeDtypeStruct((bsz, t, d), F32),
        scratch_shapes=[pltpu.VMEM((nb * tt, d), BF16)],
        compiler_params=_params(3),
        name="merge_out",
    )(a, o, z, z, wc, wa, wo, x, mod3)


FFN_PAD = SUBLANES


def _ffn_kernel(x_ref, sc_ref, sh_ref, g2_ref, ng_ref, wg_ref, wv_ref, cw_ref, cb_ref, wd_ref, ctx_ref,
                y_ref, nf_ref, h_ref, ubuf, halo):
    nb, tt, d = x_ref.shape
    tf = wg_ref.shape[1]
    kw = cw_ref.shape[0]
    t = pl.program_id(1)
    f = pl.program_id(2)

    @pl.when(f == 0)
    def _():
        h = _adaln(x_ref[...], ng_ref[...], sc_ref[...], sh_ref[...])
        h_ref[...] = h.reshape(nb * tt, d).astype(BF16)

    h = h_ref[...]
    ug = jnp.dot(h, wg_ref[...], preferred_element_type=F32).reshape(nb, tt, tf)
    uv = jnp.dot(h, wv_ref[...], preferred_element_type=F32)

    @pl.when(t == 0)
    def _():
        ubuf[:, 0:FFN_PAD, :] = ctx_ref[...]

    @pl.when(t > 0)
    def _():
        ubuf[:, 0:FFN_PAD, :] = halo[f]

    ubuf[:, FFN_PAD:FFN_PAD + tt, :] = ug
    tail = ubuf[:, tt:tt + FFN_PAD, :]
    halo[f] = tail
    nf_ref[:, :, pl.ds(pl.multiple_of(f * tf, tf), tf)] = tail

    conv = jnp.broadcast_to(cb_ref[...].reshape(1, 1, tf), (nb, tt, tf))
    for k in range(kw):
        lead = FFN_PAD - (kw - 1) + k
        conv = conv + ubuf[:, lead:lead + tt, :] * cw_ref[k:k + 1, :].reshape(1, 1, tf)
    act = (conv * _sigmoid(conv)).reshape(nb * tt, tf) * uv
    part = jnp.dot(act.astype(BF16), wd_ref[...], preferred_element_type=F32).reshape(nb, tt, d)

    @pl.when(f == 0)
    def _():
        y_ref[...] = part

    @pl.when(f > 0)
    def _():
        y_ref[...] += part

    @pl.when(f == pl.num_programs(2) - 1)
    def _():
        y_ref[...] = x_ref[...] + g2_ref[...] * y_ref[...]


def _ffn(x, mod3, row0, norm_g, w_up, conv_w, conv_b, w_down, ctx_pad, nb, tt, tf):
    bsz, t, d = x.shape
    dff = w_down.shape[0]
    kw = conv_w.shape[0]
    nf = dff // tf
    assert dff % tf == 0
    return pl.pallas_call(
        _ffn_kernel,
        grid=(bsz // nb, t // tt, nf),
        in_specs=[pl.BlockSpec((nb, tt, d), lambda i, s, f: (i, s, 0)),
                  pl.BlockSpec((nb, 1, d), lambda i, s, f: (row0 // nb + i, 0, 4)),
                  pl.BlockSpec((nb, 1, d), lambda i, s, f: (row0 // nb + i, 0, 3)),
                  pl.BlockSpec((nb, 1, d), lambda i, s, f: (row0 // nb + i, 0, 5)),
                  pl.BlockSpec((1, d), lambda i, s, f: (0, 0)),
                  pl.BlockSpec((d, tf), lambda i, s, f: (0, f)),
                  pl.BlockSpec((d, tf), lambda i, s, f: (0, nf + f)),
                  pl.BlockSpec((kw, tf), lambda i, s, f: (0, f)),
                  pl.BlockSpec((1, tf), lambda i, s, f: (0, f)),
                  pl.BlockSpec((tf, d), lambda i, s, f: (f, 0)),
                  pl.BlockSpec((nb, FFN_PAD, tf), lambda i, s, f: (i, 0, f))],
        out_specs=[pl.BlockSpec((nb, tt, d), lambda i, s, f: (i, s, 0)),
                   pl.BlockSpec((nb, FFN_PAD, dff), lambda i, s, f: (i, 0, 0))],
        out_shape=[jax.ShapeDtypeStruct((bsz, t, d), F32),
                   jax.ShapeDtypeStruct((bsz, FFN_PAD, dff), F32)],
        scratch_shapes=[pltpu.VMEM((nb * tt, d), BF16),
                        pltpu.VMEM((nb, FFN_PAD + tt, tf), F32),
                        pltpu.VMEM((nf, nb, FFN_PAD, tf), F32)],
        compiler_params=_params(3),
        name="conv_ffn",
    )(x, mod3, mod3, mod3, norm_g, w_up, w_up, conv_w, conv_b, w_down, ctx_pad)


def _rope_tables(pos):
    half = HEAD_DIM // 2
    inv_freq = 1.0 / (ROPE_THETA ** (jnp.arange(half, dtype=F32) / half))
    ang = pos.astype(F32)[:, None] * inv_freq[None, :]
    cos = jnp.cos(ang)
    sin = jnp.sin(ang)
    reps = LANES // HEAD_DIM
    return (jnp.tile(jnp.concatenate([cos, cos], axis=-1), (1, reps)),
            jnp.tile(jnp.concatenate([-sin, sin], axis=-1), (1, reps)))


def _front_pad(ctx, rows):
    return jnp.pad(ctx, ((0, 0), (rows - ctx.shape[1], 0), (0, 0)))


def _layer(x, mod3, row0, pos, conv_ctx, k_cache, v_cache, ffn_ctx, mask_prefix, p, tiles):
    bsz, t, d = x.shape
    c = p["conv_w"].shape[1]
    kvw = k_cache.shape[2] * k_cache.shape[3]
    aw = p["attn_o_w"].shape[0]
    nb, tt, tq = tiles["nb"], tiles["tt"], tiles["tq"]
    assert row0 % nb == 0 and bsz % nb == 0 and t % tt == 0 and t % tq == 0

    z = _in_proj(x, mod3, row0, p["norm1_g"], p["w_in"], p["b_in"], nb, tt, tiles["tn_in"])
    a, nc = _conv_branch(z, _front_pad(conv_ctx, CONV_PAD), p["conv_w"], p["conv_b"], p["ln_g"], p["ln_b"],
                         nb, tt)
    cos, sin = _rope_tables(pos)
    reps = LANES // HEAD_DIM
    o, nk, nv = _attention(z, k_cache.reshape(bsz, -1, kvw), v_cache.reshape(bsz, -1, kvw), cos, sin,
                           jnp.tile(p["q_norm_g"], reps)[None, :], jnp.tile(p["k_norm_g"], reps)[None, :],
                           p["sinks"], 2 * c, aw, kvw, tq, mask_prefix)
    gc_col = 2 * c + aw + 2 * kvw
    x1 = _merge(a, o, z, gc_col, gc_col + d, p["conv_out_w"], p["attn_o_w"], p["w_out"], x, mod3, row0,
                nb, tt, tiles["tn_merge"])
    y, nf = _ffn(x1, mod3, row0, p["norm2_g"], p["ffn_up_w"], p["ffn_conv_w"], p["ffn_conv_b"],
                 p["ffn_down_w"], _front_pad(ffn_ctx, FFN_PAD), nb, tt, tiles["tf"])
    kw = p["conv_w"].shape[0]
    fkw = p["ffn_conv_w"].shape[0]
    return (y, nc[:, CONV_PAD - (kw - 1):], nk.reshape(k_cache.shape), nv.reshape(v_cache.shape),
            nf[:, FFN_PAD - (fkw - 1):])


def kernel(x_prompt, x_sample, c_prompt, c_sample, cache_conv, cache_k, cache_v, cache_ffn_conv, mod_w, mod_b, norm1_g, w_in, b_in, conv_w, conv_b, ln_g, ln_b, conv_out_w, q_norm_g, k_norm_g, sinks, attn_o_w, w_out, norm2_g, ffn_up_w, ffn_conv_w, ffn_conv_b, ffn_down_w):
    depth = mod_w.shape[0]
    bp, tp, d = x_prompt.shape
    bs, ts, _ = x_sample.shape
    pos_p = jnp.arange(tp)
    pos_s = PAST_LEN + jnp.arange(ts)
    yp, ys = x_prompt, x_sample
    outs = [[] for _ in range(8)]
    tiles_p = dict(nb=1, tt=min(512, tp), tq=min(512, tp), tn_in=768, tn_merge=512, tf=512)
    tiles_s = dict(nb=min(8, bs), tt=ts, tq=ts, tn_in=768, tn_merge=512, tf=512)
    for l in range(depth):
        p = dict(norm1_g=norm1_g[l][None, :], w_in=w_in[l].astype(BF16), b_in=b_in[l][None, :],
                 conv_w=conv_w[l], conv_b=conv_b[l][None, :], ln_g=ln_g[l][None, :], ln_b=ln_b[l][None, :],
                 conv_out_w=conv_out_w[l].astype(BF16), q_norm_g=q_norm_g[l], k_norm_g=k_norm_g[l],
                 sinks=sinks[l], attn_o_w=attn_o_w[l].astype(BF16), w_out=w_out[l].astype(BF16),
                 norm2_g=norm2_g[l][None, :], ffn_up_w=ffn_up_w[l].astype(BF16), ffn_conv_w=ffn_conv_w[l],
                 ffn_conv_b=ffn_conv_b[l][None, :], ffn_down_w=ffn_down_w[l].astype(BF16))
        c_all = jnp.concatenate([c_prompt, c_sample], axis=0)
        mod3 = _mod(c_all, mod_w[l], mod_b[l][None, :])[:, None, :]
        zeros_conv = jnp.zeros((bp,) + cache_conv.shape[2:], F32)
        zeros_kv = jnp.zeros((bp,) + cache_k.shape[2:], F32)
        zeros_ffn = jnp.zeros((bp,) + cache_ffn_conv.shape[2:], F32)
        yp, nc_p, nk_p, nv_p, nf_p = _layer(yp, mod3, 0, pos_p, zeros_conv, zeros_kv, zeros_kv, zeros_ffn,
                                            True, p, tiles_p)
        ys, nc_s, nk_s, nv_s, nf_s = _layer(ys, mod3, bp, pos_s, cache_conv[l], cache_k[l], cache_v[l],
                                            cache_ffn_conv[l], False, p, tiles_s)
        for lst, val in zip(outs, (nc_p, nc_s, nk_p, nk_s, nv_p, nv_s, nf_p, nf_s)):
            lst.append(val)
    return (yp, ys) + tuple(jnp.stack(o) for o in outs)
```

```python
import functools

import jax
import jax.numpy as jnp
from jax import lax
from jax.experimental import pallas as pl
from jax.experimental.pallas import tpu as pltpu

CHUNK = 64
HEAD_DIM = 64
WINDOW = 128
PAST_LEN = 1024
ROPE_THETA = 10000.0
EPS = 1e-6
NEG_INF = -1e30
LANES = 128
SUBLANES = 8
VMEM_LIMIT_BYTES = 60 * 1024 * 1024

F32 = jnp.float32
BF16 = jnp.bfloat16


def _params(n_axes):
    return pltpu.CompilerParams(dimension_semantics=("arbitrary",) * n_axes,
                                vmem_limit_bytes=VMEM_LIMIT_BYTES)


def _sigmoid(x):
    return 1.0 / (1.0 + jnp.exp(-x))


def _mod_kernel(c_ref, w_ref, b_ref, o_ref):
    c = c_ref[...]
    a = (c * _sigmoid(c)).astype(BF16)
    o_ref[...] = jnp.dot(a, w_ref[...].astype(BF16), preferred_element_type=F32) + b_ref[...]


def _mod(c_all, mod_w, mod_b, tn=1024):
    m, d = c_all.shape
    n = mod_w.shape[1]
    return pl.pallas_call(
        _mod_kernel,
        grid=(n // tn,),
        in_specs=[pl.BlockSpec((m, d), lambda j: (0, 0)),
                  pl.BlockSpec((d, tn), lambda j: (0, j)),
                  pl.BlockSpec((1, tn), lambda j: (0, j))],
        out_specs=pl.BlockSpec((m, tn), lambda j: (0, j)),
        out_shape=jax.ShapeDtypeStruct((m, n), F32),
        compiler_params=_params(1),
        name="mod",
    )(c_all, mod_w, mod_b)


def _adaln(x, g, sc, sh):
    ms = jnp.mean(x * x, axis=-1, keepdims=True)
    y = x * lax.rsqrt(ms + EPS) * g
    return y * (1.0 + sc) + sh


def _in_kernel(x_ref, sc_ref, sh_ref, g_ref, w_ref, b_ref, o_ref, h_ref):
    nb, tt, d = x_ref.shape

    @pl.when(pl.program_id(2) == 0)
    def _():
        h = _adaln(x_ref[...], g_ref[...], sc_ref[...], sh_ref[...])
        h_ref[...] = h.reshape(nb * tt, d).astype(BF16)

    z = jnp.dot(h_ref[...], w_ref[...], preferred_element_type=F32) + b_ref[...]
    o_ref[...] = z.reshape(nb, tt, -1).astype(o_ref.dtype)


def _in_proj(x, mod3, row0, norm_g, w, b, nb, tt, tn):
    bsz, t, d = x.shape
    n = w.shape[1]
    return pl.pallas_call(
        _in_kernel,
        grid=(bsz // nb, t // tt, n // tn),
        in_specs=[pl.BlockSpec((nb, tt, d), lambda i, s, j: (i, s, 0)),
                  pl.BlockSpec((nb, 1, d), lambda i, s, j: (row0 // nb + i, 0, 1)),
                  pl.BlockSpec((nb, 1, d), lambda i, s, j: (row0 // nb + i, 0, 0)),
                  pl.BlockSpec((1, d), lambda i, s, j: (0, 0)),
                  pl.BlockSpec((d, tn), lambda i, s, j: (0, j)),
                  pl.BlockSpec((1, tn), lambda i, s, j: (0, j))],
        out_specs=pl.BlockSpec((nb, tt, tn), lambda i, s, j: (i, s, j)),
        out_shape=jax.ShapeDtypeStruct((bsz, t, n), BF16),
        scratch_shapes=[pltpu.VMEM((nb * tt, d), BF16)],
        compiler_params=_params(3),
        name="in_proj",
    )(x, mod3, mod3, norm_g, w, b)


CONV_PAD = 32
CONV_STEPS = 16


def _slab_rows(rows):
    pitch = -(-rows // SUBLANES) * SUBLANES
    return pitch if (pitch // SUBLANES) % 2 else pitch + SUBLANES


def _conv_kernel(za_ref, zb_ref, ctx_ref, w_ref, b_ref, lg_ref, lb_ref, a_ref, nc_ref, gbuf, dwbuf):
    nb, tt, c = za_ref.shape
    nct = c // LANES
    kw = w_ref.shape[0]
    tp = gbuf.shape[1] // nct
    tp2 = dwbuf.shape[1] // nct
    lead = CONV_PAD - (kw - 1)
    t = pl.program_id(1)

    @pl.when(t == 0)
    def _():
        for j in range(nct):
            gbuf[:, j * tp:j * tp + CONV_PAD, :] = ctx_ref[:, :, j * LANES:(j + 1) * LANES]

    @pl.when(t > 0)
    def _():
        for j in range(nct):
            gbuf[:, j * tp:j * tp + CONV_PAD, :] = gbuf[:, j * tp + tt:j * tp + tt + CONV_PAD, :]

    for j in range(nct):
        ls = slice(j * LANES, (j + 1) * LANES)
        glu = za_ref[:, :, ls].astype(F32) * _sigmoid(zb_ref[:, :, ls].astype(F32))
        gbuf[:, j * tp + CONV_PAD:j * tp + CONV_PAD + tt, :] = glu
        nc_ref[:, :, ls] = gbuf[:, j * tp + tt:j * tp + tt + CONV_PAD, :]

    w = [w_ref[k] for k in range(kw)]
    bias = b_ref[...]
    nblk = tt // CONV_STEPS

    def body(i, carry):
        n = i // nblk
        t0 = (i % nblk) * CONV_STEPS
        acc = [bias] * CONV_STEPS
        for m in range(CONV_STEPS + kw - 1):
            g = gbuf[n, pl.ds(t0 + lead + m, SUBLANES, stride=tp), :]
            for s in range(max(0, m - (kw - 1)), min(CONV_STEPS - 1, m) + 1):
                acc[s] = acc[s] + g * w[m - s]
        for s in range(CONV_STEPS):
            dwbuf[n, pl.ds(t0 + s, SUBLANES, stride=tp2), :] = acc[s]
        return carry

    lax.fori_loop(0, nb * nblk, body, 0)

    slabs = [dwbuf[:, j * tp2:j * tp2 + tt, :] for j in range(nct)]
    mu = jnp.sum(functools.reduce(jnp.add, slabs), axis=-1, keepdims=True) * (1.0 / c)
    cen = [d - mu for d in slabs]
    var = jnp.sum(functools.reduce(jnp.add, [x * x for x in cen]), axis=-1, keepdims=True) * (1.0 / c)
    inv = lax.rsqrt(var + EPS)
    for j in range(nct):
        ls = slice(j * LANES, (j + 1) * LANES)
        y = cen[j] * inv * lg_ref[:, ls] + lb_ref[:, ls]
        a_ref[:, :, ls] = (y * _sigmoid(y)).astype(a_ref.dtype)


def _conv_branch(z, ctx_pad, conv_w, conv_b, ln_g, ln_b, nb, tt):
    bsz, t, _ = z.shape
    kw, c = conv_w.shape
    nct = c // LANES
    assert nct == SUBLANES and tt % CONV_STEPS == 0
    tp = _slab_rows(CONV_PAD + tt)
    tp2 = _slab_rows(tt)
    return pl.pallas_call(
        _conv_kernel,
        grid=(bsz // nb, t // tt),
        in_specs=[pl.BlockSpec((nb, tt, c), lambda i, s: (i, s, 0)),
                  pl.BlockSpec((nb, tt, c), lambda i, s: (i, s, 1)),
                  pl.BlockSpec((nb, CONV_PAD, c), lambda i, s: (i, 0, 0)),
                  pl.BlockSpec((kw, nct, LANES), lambda i, s: (0, 0, 0)),
                  pl.BlockSpec((nct, LANES), lambda i, s: (0, 0)),
                  pl.BlockSpec((1, c), lambda i, s: (0, 0)),
                  pl.BlockSpec((1, c), lambda i, s: (0, 0))],
        out_specs=[pl.BlockSpec((nb, tt, c), lambda i, s: (i, s, 0)),
                   pl.BlockSpec((nb, CONV_PAD, c), lambda i, s: (i, 0, 0))],
        out_shape=[jax.ShapeDtypeStruct((bsz, t, c), BF16),
                   jax.ShapeDtypeStruct((bsz, CONV_PAD, c), F32)],
        scratch_shapes=[pltpu.VMEM((nb, nct * tp, LANES), F32),
                        pltpu.VMEM((nb, nct * tp2, LANES), F32)],
        compiler_params=_params(2),
        name="conv_branch",
    )(z, z, ctx_pad, conv_w.reshape(kw, nct, LANES), conv_b.reshape(nct, LANES), ln_g, ln_b)


def _pair_norm_rope(x, g, cos, sin):
    lane = lax.broadcasted_iota(jnp.int32, x.shape, 1)
    first = lane < HEAD_DIM
    sq = x * x
    s_a = jnp.sum(jnp.where(first, sq, 0.0), axis=-1, keepdims=True)
    s_b = jnp.sum(jnp.where(first, 0.0, sq), axis=-1, keepdims=True)
    ms = jnp.where(first, s_a, s_b) * (1.0 / HEAD_DIM)
    y = x * lax.rsqrt(ms + EPS) * g
    lower = (lane & (HEAD_DIM - 1)) < (HEAD_DIM // 2)
    partner = jnp.where(lower, pltpu.roll(y, LANES - HEAD_DIM // 2, 1), pltpu.roll(y, HEAD_DIM // 2, 1))
    return y * cos + partner * sin


def _attn_kernel(sink_ref, q_ref, k_ref, v_ref, kc_ref, vc_ref, cos_ref, sin_ref, qg_ref, kg_ref,
                 o_ref, nk_ref, nv_ref, kf, vf, kx, vx, qs, *, mask_prefix):
    tq = q_ref.shape[1]
    kvw = k_ref.shape[2]
    n_kv = kvw // HEAD_DIM
    t = pl.program_id(1)
    cos = cos_ref[...]
    sin = sin_ref[...]

    @pl.when(t == 0)
    def _():
        kf[0:WINDOW, :] = kc_ref[0]
        vf[0:WINDOW, :] = vc_ref[0]

    @pl.when(t > 0)
    def _():
        kf[0:WINDOW, :] = kf[tq:tq + WINDOW, :]
        vf[0:WINDOW, :] = vf[tq:tq + WINDOW, :]

    k_new = k_ref[0].astype(F32)
    for j in range(kvw // LANES):
        ls = slice(j * LANES, (j + 1) * LANES)
        kf[WINDOW:WINDOW + tq, ls] = _pair_norm_rope(k_new[:, ls], kg_ref[...], cos, sin)
    vf[WINDOW:WINDOW + tq, :] = v_ref[0].astype(F32)
    nk_ref[0] = kf[tq:tq + WINDOW, :]
    nv_ref[0] = vf[tq:tq + WINDOW, :]

    lane = lax.broadcasted_iota(jnp.int32, (WINDOW + tq, LANES), 1)
    first = lane < HEAD_DIM
    for src, dst in ((kf, kx), (vf, vx)):
        for j in range(kvw // LANES):
            tile = src[:, j * LANES:(j + 1) * LANES]
            swapped = pltpu.roll(tile, HEAD_DIM, 1)
            variants = (jnp.where(first, tile, 0.0), jnp.where(first, 0.0, swapped),
                        jnp.where(first, swapped, 0.0), jnp.where(first, 0.0, tile))
            for m, val in enumerate(variants):
                dst[:, (4 * j + m) * LANES:(4 * j + m + 1) * LANES] = val.astype(BF16)

    scale = HEAD_DIM ** -0.5
    for j in range(q_ref.shape[2] // LANES):
        ls = slice(j * LANES, (j + 1) * LANES)
        qn = _pair_norm_rope(q_ref[0, :, ls].astype(F32), qg_ref[...], cos, sin)
        qs[:, ls] = (qn * scale).astype(BF16)

    n_keys = WINDOW + CHUNK
    group = q_ref.shape[2] // kvw
    rows = group * CHUNK
    row = lax.broadcasted_iota(jnp.int32, (rows, 1), 0)
    col = lax.broadcasted_iota(jnp.int32, (rows, n_keys), 1)
    dn = (((1,), (1,)), ((), ()))

    def chunk_body(c, carry):
        r0 = pl.multiple_of(c * CHUNK, CHUNK)
        for h in range(n_kv):
            q2 = jnp.concatenate([qs[pl.ds(r0, CHUNK), (2 * h) * LANES:(2 * h + 1) * LANES],
                                  qs[pl.ds(r0, CHUNK), (2 * h + 1) * LANES:(2 * h + 2) * LANES]], axis=0)
            k_lo = kx[pl.ds(r0, n_keys), (2 * h) * LANES:(2 * h + 1) * LANES]
            k_hi = kx[pl.ds(r0, n_keys), (2 * h + 1) * LANES:(2 * h + 2) * LANES]
            s = jnp.concatenate([lax.dot_general(q2, k_lo, dn, preferred_element_type=F32),
                                 lax.dot_general(q2, k_hi, dn, preferred_element_type=F32)], axis=0)
            sink = jnp.where(row < CHUNK, sink_ref[4 * h],
                             jnp.where(row < 2 * CHUNK, sink_ref[4 * h + 2],
                                       jnp.where(row < 3 * CHUNK, sink_ref[4 * h + 1], sink_ref[4 * h + 3])))
            if mask_prefix:
                first_key = WINDOW - (t * tq + r0)
                s = jnp.where(col >= first_key, s, NEG_INF)
            m = jnp.maximum(jnp.max(s, axis=-1, keepdims=True), sink)
            p = jnp.exp(s - m)
            denom = jnp.sum(p, axis=-1, keepdims=True) + jnp.exp(sink - m)
            p = (p / denom).astype(BF16)
            v_lo = vx[pl.ds(r0, n_keys), (2 * h) * LANES:(2 * h + 1) * LANES]
            v_hi = vx[pl.ds(r0, n_keys), (2 * h + 1) * LANES:(2 * h + 2) * LANES]
            o2 = (jnp.dot(p[0:2 * CHUNK], v_lo, preferred_element_type=F32)
                  + jnp.dot(p[2 * CHUNK:4 * CHUNK], v_hi, preferred_element_type=F32))
            o_ref[0, pl.ds(r0, CHUNK), (2 * h) * LANES:(2 * h + 1) * LANES] = o2[0:CHUNK].astype(o_ref.dtype)
            o_ref[0, pl.ds(r0, CHUNK), (2 * h + 1) * LANES:(2 * h + 2) * LANES] = o2[CHUNK:2 * CHUNK].astype(o_ref.dtype)
        return carry

    lax.fori_loop(0, tq // CHUNK, chunk_body, 0)


def _attention(z, k_cache, v_cache, cos, sin, q_g, k_g, sinks, q_col, aw, kvw, tq, mask_prefix):
    bsz, t, _ = z.shape
    assert q_col % aw == 0 and (q_col + aw) % kvw == 0 and aw // kvw == 4 and kvw % LANES == 0
    k_blk = (q_col + aw) // kvw
    return pl.pallas_call(
        functools.partial(_attn_kernel, mask_prefix=mask_prefix),
        grid=(bsz, t // tq),
        in_specs=[pl.BlockSpec(memory_space=pltpu.SMEM),
                  pl.BlockSpec((1, tq, aw), lambda i, s: (i, s, q_col // aw)),
                  pl.BlockSpec((1, tq, kvw), lambda i, s: (i, s, k_blk)),
                  pl.BlockSpec((1, tq, kvw), lambda i, s: (i, s, k_blk + 1)),
                  pl.BlockSpec((1, WINDOW, kvw), lambda i, s: (i, 0, 0)),
                  pl.BlockSpec((1, WINDOW, kvw), lambda i, s: (i, 0, 0)),
                  pl.BlockSpec((tq, LANES), lambda i, s: (s, 0)),
                  pl.BlockSpec((tq, LANES), lambda i, s: (s, 0)),
                  pl.BlockSpec((1, LANES), lambda i, s: (0, 0)),
                  pl.BlockSpec((1, LANES), lambda i, s: (0, 0))],
        out_specs=[pl.BlockSpec((1, tq, aw), lambda i, s: (i, s, 0)),
                   pl.BlockSpec((1, WINDOW, kvw), lambda i, s: (i, 0, 0)),
                   pl.BlockSpec((1, WINDOW, kvw), lambda i, s: (i, 0, 0))],
        scratch_shapes=[pltpu.VMEM((WINDOW + tq, kvw), F32),
                        pltpu.VMEM((WINDOW + tq, kvw), F32),
                        pltpu.VMEM((WINDOW + tq, 4 * kvw), BF16),
                        pltpu.VMEM((WINDOW + tq, 4 * kvw), BF16),
                        pltpu.VMEM((tq, aw), BF16)],
        out_shape=[jax.ShapeDtypeStruct((bsz, t, aw), BF16),
                   jax.ShapeDtypeStruct((bsz, WINDOW, kvw), F32),
                   jax.ShapeDtypeStruct((bsz, WINDOW, kvw), F32)],
        compiler_params=_params(2),
        name="attention",
    )(sinks, z, z, z, k_cache, v_cache, cos, sin, q_g, k_g)


def _merge_kernel(a_ref, o_ref, gc_ref, ga_ref, wc_ref, wa_ref, wo_ref, x_ref, g1_ref, y_ref, mg):
    nb, tt, d = x_ref.shape
    tn = wc_ref.shape[1]
    n = pl.program_id(2)
    a = a_ref[...].reshape(nb * tt, -1)
    o = o_ref[...].reshape(nb * tt, -1)
    yc = jnp.dot(a, wc_ref[...], preferred_element_type=F32)
    ya = jnp.dot(o, wa_ref[...], preferred_element_type=F32)
    gc = gc_ref[...].astype(F32).reshape(nb * tt, tn)
    ga = ga_ref[...].astype(F32).reshape(nb * tt, tn)
    merged = _sigmoid(gc) * yc + _sigmoid(ga) * ya
    mg[:, pl.ds(pl.multiple_of(n * tn, tn), tn)] = merged.astype(BF16)

    @pl.when(n == pl.num_programs(2) - 1)
    def _():
        proj = jnp.dot(mg[...], wo_ref[...], preferred_element_type=F32).reshape(nb, tt, d)
        y_ref[...] = x_ref[...] + g1_ref[...] * proj


def _merge(a, o, z, gc_col, ga_col, wc, wa, wo, x, mod3, row0, nb, tt, tn):
    bsz, t, d = x.shape
    c = a.shape[2]
    aw = o.shape[2]
    assert gc_col % tn == 0 and ga_col % tn == 0 and d % tn == 0
    return pl.pallas_call(
        _merge_kernel,
        grid=(bsz // nb, t // tt, d // tn),
        in_specs=[pl.BlockSpec((nb, tt, c), lambda i, s, n: (i, s, 0)),
                  pl.BlockSpec((nb, tt, aw), lambda i, s, n: (i, s, 0)),
                  pl.BlockSpec((nb, tt, tn), lambda i, s, n: (i, s, gc_col // tn + n)),
                  pl.BlockSpec((nb, tt, tn), lambda i, s, n: (i, s, ga_col // tn + n)),
                  pl.BlockSpec((c, tn), lambda i, s, n: (0, n)),
                  pl.BlockSpec((aw, tn), lambda i, s, n: (0, n)),
                  pl.BlockSpec((d, d), lambda i, s, n: (0, 0)),
                  pl.BlockSpec((nb, tt, d), lambda i, s, n: (i, s, 0)),
                  pl.BlockSpec((nb, 1, d), lambda i, s, n: (row0 // nb + i, 0, 2))],
        out_specs=pl.BlockSpec((nb, tt, d), lambda i, s, n: (i, s, 0)),
        out_shape=jax.ShapeDtypeStruct((bsz, t, d), F32),
        scratch_shapes=[pltpu.VMEM((nb * tt, d), BF16)],
        compiler_params=_params(3),
        name="merge_out",
    )(a, o, z, z, wc, wa, wo, x, mod3)


FFN_PAD = SUBLANES
FFN_SUB = 256


def _ffn_kernel(x_ref, sc_ref, sh_ref, g2_ref, ng_ref, wg_ref, wv_ref, cw_ref, cb_ref, wd_ref, ctx_ref,
                y_ref, nf_ref, h_ref, ubuf, vbuf, halo):
    nb, tt, d = x_ref.shape
    tf = wg_ref.shape[1]
    kw = cw_ref.shape[0]
    t = pl.program_id(1)
    f = pl.program_id(2)

    @pl.when(f == 0)
    def _():
        h = _adaln(x_ref[...], ng_ref[...], sc_ref[...], sh_ref[...])
        h_ref[...] = h.reshape(nb * tt, d).astype(BF16)

    @pl.when(t == 0)
    def _():
        ubuf[:, 0:FFN_PAD, :] = ctx_ref[...]

    @pl.when(t > 0)
    def _():
        ubuf[:, 0:FFN_PAD, :] = halo[f]

    h = h_ref[...]
    part = None
    for c0 in range(0, tf, FFN_SUB):
        cs = slice(c0, c0 + FFN_SUB)
        ug = jnp.dot(h, wg_ref[:, cs], preferred_element_type=F32)
        ubuf[:, FFN_PAD:FFN_PAD + tt, cs] = ug.reshape(nb, tt, FFN_SUB)
        vbuf[:, cs] = jnp.dot(h, wv_ref[:, cs], preferred_element_type=F32)
    for c0 in range(0, tf, FFN_SUB):
        cs = slice(c0, c0 + FFN_SUB)
        uv = vbuf[:, cs]
        conv = jnp.broadcast_to(cb_ref[:, cs].reshape(1, 1, FFN_SUB), (nb, tt, FFN_SUB))
        for k in range(kw):
            lead = FFN_PAD - (kw - 1) + k
            conv = conv + ubuf[:, lead:lead + tt, cs] * cw_ref[k:k + 1, cs].reshape(1, 1, FFN_SUB)
        act = (conv * _sigmoid(conv)).reshape(nb * tt, FFN_SUB) * uv
        contrib = jnp.dot(act.astype(BF16), wd_ref[cs, :], preferred_element_type=F32)
        part = contrib if part is None else part + contrib
    part = part.reshape(nb, tt, d)

    tail = ubuf[:, tt:tt + FFN_PAD, :]
    halo[f] = tail
    nf_ref[:, :, pl.ds(pl.multiple_of(f * tf, tf), tf)] = tail

    @pl.when(f == 0)
    def _():
        y_ref[...] = part

    @pl.when(jnp.logical_and(f > 0, f < pl.num_programs(2) - 1))
    def _():
        y_ref[...] += part

    @pl.when(f == pl.num_programs(2) - 1)
    def _():
        y_ref[...] = x_ref[...] + g2_ref[...] * (y_ref[...] + part)


def _ffn(x, mod3, row0, norm_g, w_up, conv_w, conv_b, w_down, ctx_pad, nb, tt, tf):
    bsz, t, d = x.shape
    dff = w_down.shape[0]
    kw = conv_w.shape[0]
    nf = dff // tf
    assert dff % tf == 0 and nf >= 2 and tf % FFN_SUB == 0
    return pl.pallas_call(
        _ffn_kernel,
        grid=(bsz // nb, t // tt, nf),
        in_specs=[pl.BlockSpec((nb, tt, d), lambda i, s, f: (i, s, 0), pipeline_mode=pl.Buffered(1)),
                  pl.BlockSpec((nb, 1, d), lambda i, s, f: (row0 // nb + i, 0, 4)),
                  pl.BlockSpec((nb, 1, d), lambda i, s, f: (row0 // nb + i, 0, 3)),
                  pl.BlockSpec((nb, 1, d), lambda i, s, f: (row0 // nb + i, 0, 5)),
                  pl.BlockSpec((1, d), lambda i, s, f: (0, 0)),
                  pl.BlockSpec((d, tf), lambda i, s, f: (0, f)),
                  pl.BlockSpec((d, tf), lambda i, s, f: (0, nf + f)),
                  pl.BlockSpec((kw, tf), lambda i, s, f: (0, f)),
                  pl.BlockSpec((1, tf), lambda i, s, f: (0, f)),
                  pl.BlockSpec((tf, d), lambda i, s, f: (f, 0)),
                  pl.BlockSpec((nb, FFN_PAD, tf), lambda i, s, f: (i, 0, f))],
        out_specs=[pl.BlockSpec((nb, tt, d), lambda i, s, f: (i, s, 0)),
                   pl.BlockSpec((nb, FFN_PAD, dff), lambda i, s, f: (i, 0, 0))],
        out_shape=[jax.ShapeDtypeStruct((bsz, t, d), F32),
                   jax.ShapeDtypeStruct((bsz, FFN_PAD, dff), F32)],
        scratch_shapes=[pltpu.VMEM((nb * tt, d), BF16),
                        pltpu.VMEM((nb, FFN_PAD + tt, tf), F32),
                        pltpu.VMEM((nb * tt, tf), F32),
                        pltpu.VMEM((nf, nb, FFN_PAD, tf), F32)],
        compiler_params=_params(3),
        name="conv_ffn",
    )(x, mod3, mod3, mod3, norm_g, w_up, w_up, conv_w, conv_b, w_down, ctx_pad)


def _rope_tables(pos):
    half = HEAD_DIM // 2
    inv_freq = 1.0 / (ROPE_THETA ** (jnp.arange(half, dtype=F32) / half))
    ang = pos.astype(F32)[:, None] * inv_freq[None, :]
    cos = jnp.cos(ang)
    sin = jnp.sin(ang)
    reps = LANES // HEAD_DIM
    return (jnp.tile(jnp.concatenate([cos, cos], axis=-1), (1, reps)),
            jnp.tile(jnp.concatenate([-sin, sin], axis=-1), (1, reps)))


def _front_pad(ctx, rows):
    return jnp.pad(ctx, ((0, 0), (rows - ctx.shape[1], 0), (0, 0)))


def _layer(x, mod3, row0, pos, conv_ctx, k_cache, v_cache, ffn_ctx, mask_prefix, p, tiles):
    bsz, t, d = x.shape
    c = p["conv_w"].shape[1]
    kvw = k_cache.shape[2] * k_cache.shape[3]
    aw = p["attn_o_w"].shape[0]
    nb, tt, tq = tiles["nb"], tiles["tt"], tiles["tq"]
    assert row0 % nb == 0 and bsz % nb == 0 and t % tt == 0 and t % tq == 0

    z = _in_proj(x, mod3, row0, p["norm1_g"], p["w_in"], p["b_in"], nb, tiles["tt_big"], tiles["tn_in"])
    a, nc = _conv_branch(z, _front_pad(conv_ctx, CONV_PAD), p["conv_w"], p["conv_b"], p["ln_g"], p["ln_b"],
                         nb, tt)
    cos, sin = _rope_tables(pos)
    reps = LANES // HEAD_DIM
    o, nk, nv = _attention(z, k_cache.reshape(bsz, -1, kvw), v_cache.reshape(bsz, -1, kvw), cos, sin,
                           jnp.tile(p["q_norm_g"], reps)[None, :], jnp.tile(p["k_norm_g"], reps)[None, :],
                           p["sinks"], 2 * c, aw, kvw, tq, mask_prefix)
    gc_col = 2 * c + aw + 2 * kvw
    x1 = _merge(a, o, z, gc_col, gc_col + d, p["conv_out_w"], p["attn_o_w"], p["w_out"], x, mod3, row0,
                nb, tt, tiles["tn_merge"])
    y, nf = _ffn(x1, mod3, row0, p["norm2_g"], p["ffn_up_w"], p["ffn_conv_w"], p["ffn_conv_b"],
                 p["ffn_down_w"], _front_pad(ffn_ctx, FFN_PAD), nb, tiles["tt_big"], tiles["tf"])
    kw = p["conv_w"].shape[0]
    fkw = p["ffn_conv_w"].shape[0]
    return (y, nc[:, CONV_PAD - (kw - 1):], nk.reshape(k_cache.shape), nv.reshape(v_cache.shape),
            nf[:, FFN_PAD - (fkw - 1):])


def kernel(x_prompt, x_sample, c_prompt, c_sample, cache_conv, cache_k, cache_v, cache_ffn_conv, mod_w, mod_b, norm1_g, w_in, b_in, conv_w, conv_b, ln_g, ln_b, conv_out_w, q_norm_g, k_norm_g, sinks, attn_o_w, w_out, norm2_g, ffn_up_w, ffn_conv_w, ffn_conv_b, ffn_down_w):
    depth = mod_w.shape[0]
    bp, tp, d = x_prompt.shape
    bs, ts, _ = x_sample.shape
    pos_p = jnp.arange(tp)
    pos_s = PAST_LEN + jnp.arange(ts)
    yp, ys = x_prompt, x_sample
    outs = [[] for _ in range(8)]
    tiles_p = dict(nb=1, tt=min(512, tp), tt_big=min(1024, tp), tq=min(512, tp), tn_in=1536, tn_merge=512,
                   tf=512)
    tiles_s = dict(nb=min(8, bs), tt=ts, tt_big=ts, tq=ts, tn_in=1536, tn_merge=512, tf=512)
    for l in range(depth):
        p = dict(norm1_g=norm1_g[l][None, :], w_in=w_in[l].astype(BF16), b_in=b_in[l][None, :],
                 conv_w=conv_w[l], conv_b=conv_b[l][None, :], ln_g=ln_g[l][None, :], ln_b=ln_b[l][None, :],
                 conv_out_w=conv_out_w[l].astype(BF16), q_norm_g=q_norm_g[l], k_norm_g=k_norm_g[l],
                 sinks=sinks[l], attn_o_w=attn_o_w[l].astype(BF16), w_out=w_out[l].astype(BF16),
                 norm2_g=norm2_g[l][None, :], ffn_up_w=ffn_up_w[l].astype(BF16), ffn_conv_w=ffn_conv_w[l],
                 ffn_conv_b=ffn_conv_b[l][None, :], ffn_down_w=ffn_down_w[l].astype(BF16))
        c_all = jnp.concatenate([c_prompt, c_sample], axis=0)
        mod3 = _mod(c_all, mod_w[l], mod_b[l][None, :])[:, None, :]
        zeros_conv = jnp.zeros((bp,) + cache_conv.shape[2:], F32)
        zeros_kv = jnp.zeros((bp,) + cache_k.shape[2:], F32)
        zeros_ffn = jnp.zeros((bp,) + cache_ffn_conv.shape[2:], F32)
        yp, nc_p, nk_p, nv_p, nf_p = _layer(yp, mod3, 0, pos_p, zeros_conv, zeros_kv, zeros_kv, zeros_ffn,
                                            True, p, tiles_p)
        ys, nc_s, nk_s, nv_s, nf_s = _layer(ys, mod3, bp, pos_s, cache_conv[l], cache_k[l], cache_v[l],
                                            cache_ffn_conv[l], False, p, tiles_s)
        for lst, val in zip(outs, (nc_p, nc_s, nk_p, nk_s, nv_p, nv_s, nf_p, nf_s)):
            lst.append(val)
    return (yp, ys) + tuple(jnp.stack(o) for o in outs)
```

```python
import functools

import jax
import jax.numpy as jnp
from jax import lax
from jax.experimental import pallas as pl
from jax.experimental.pallas import tpu as pltpu

CHUNK = 64
HEAD_DIM = 64
WINDOW = 128
PAST_LEN = 1024
ROPE_THETA = 10000.0
EPS = 1e-6
NEG_INF = -1e30
LANES = 128
SUBLANES = 8
VMEM_LIMIT_BYTES = 60 * 1024 * 1024

F32 = jnp.float32
BF16 = jnp.bfloat16


def _params(n_axes):
    return pltpu.CompilerParams(dimension_semantics=("arbitrary",) * n_axes,
                                vmem_limit_bytes=VMEM_LIMIT_BYTES)


def _sigmoid(x):
    return 1.0 / (1.0 + jnp.exp(-x))


def _mod_kernel(c_ref, w_ref, b_ref, o_ref):
    c = c_ref[...]
    a = (c * _sigmoid(c)).astype(BF16)
    o_ref[...] = jnp.dot(a, w_ref[...].astype(BF16), preferred_element_type=F32) + b_ref[...]


def _mod(c_all, mod_w, mod_b, tn=1024):
    m, d = c_all.shape
    n = mod_w.shape[1]
    return pl.pallas_call(
        _mod_kernel,
        grid=(n // tn,),
        in_specs=[pl.BlockSpec((m, d), lambda j: (0, 0)),
                  pl.BlockSpec((d, tn), lambda j: (0, j)),
                  pl.BlockSpec((1, tn), lambda j: (0, j))],
        out_specs=pl.BlockSpec((m, tn), lambda j: (0, j)),
        out_shape=jax.ShapeDtypeStruct((m, n), F32),
        compiler_params=_params(1),
        name="mod",
    )(c_all, mod_w, mod_b)


PROLOGUE_PARTS = 4


def _row_parts(nb, tt, parts):
    if nb >= parts:
        step = nb // parts
        return [(slice(i * step, (i + 1) * step), slice(0, tt)) for i in range(parts)]
    step = tt // parts
    return [(slice(b, b + 1), slice(i * step, (i + 1) * step)) for b in range(nb) for i in range(parts)]


def _adaln(x, g, sc, sh):
    ms = jnp.mean(x * x, axis=-1, keepdims=True)
    y = x * lax.rsqrt(ms + EPS) * g
    return y * (1.0 + sc) + sh


def _in_kernel(x_ref, sc_ref, sh_ref, g_ref, w_ref, b_ref, o_ref, h_ref):
    nb, tt, d = x_ref.shape
    j = pl.program_id(2)

    @pl.when(j == 0)
    def _():
        for bs, ts in _row_parts(nb, tt, PROLOGUE_PARTS):
            rows = (bs.stop - bs.start) * (ts.stop - ts.start)
            r0 = bs.start * tt + ts.start
            h = _adaln(x_ref[bs, ts, :], g_ref[...], sc_ref[bs], sh_ref[bs]).reshape(rows, d).astype(BF16)
            h_ref[r0:r0 + rows, :] = h
            z = jnp.dot(h, w_ref[...], preferred_element_type=F32) + b_ref[...]
            o_ref[bs, ts, :] = z.reshape(bs.stop - bs.start, ts.stop - ts.start, -1).astype(o_ref.dtype)

    @pl.when(j > 0)
    def _():
        z = jnp.dot(h_ref[...], w_ref[...], preferred_element_type=F32) + b_ref[...]
        o_ref[...] = z.reshape(nb, tt, -1).astype(o_ref.dtype)


def _in_proj(x, mod3, row0, norm_g, w, b, nb, tt, tn):
    bsz, t, d = x.shape
    n = w.shape[1]
    return pl.pallas_call(
        _in_kernel,
        grid=(bsz // nb, t // tt, n // tn),
        in_specs=[pl.BlockSpec((nb, tt, d), lambda i, s, j: (i, s, 0)),
                  pl.BlockSpec((nb, 1, d), lambda i, s, j: (row0 // nb + i, 0, 1)),
                  pl.BlockSpec((nb, 1, d), lambda i, s, j: (row0 // nb + i, 0, 0)),
                  pl.BlockSpec((1, d), lambda i, s, j: (0, 0)),
                  pl.BlockSpec((d, tn), lambda i, s, j: (0, j)),
                  pl.BlockSpec((1, tn), lambda i, s, j: (0, j))],
        out_specs=pl.BlockSpec((nb, tt, tn), lambda i, s, j: (i, s, j)),
        out_shape=jax.ShapeDtypeStruct((bsz, t, n), BF16),
        scratch_shapes=[pltpu.VMEM((nb * tt, d), BF16)],
        compiler_params=_params(3),
        name="in_proj",
    )(x, mod3, mod3, norm_g, w, b)


CONV_PAD = 32
CONV_STEPS = 16


def _slab_rows(rows):
    pitch = -(-rows // SUBLANES) * SUBLANES
    return pitch if (pitch // SUBLANES) % 2 else pitch + SUBLANES


def _conv_kernel(za_ref, zb_ref, ctx_ref, w_ref, b_ref, lg_ref, lb_ref, a_ref, nc_ref, gbuf, dwbuf):
    nb, tt, c = za_ref.shape
    nct = c // LANES
    kw = w_ref.shape[0]
    tp = gbuf.shape[1] // nct
    tp2 = dwbuf.shape[1] // nct
    lead = CONV_PAD - (kw - 1)
    t = pl.program_id(1)

    @pl.when(t == 0)
    def _():
        for j in range(nct):
            gbuf[:, j * tp:j * tp + CONV_PAD, :] = ctx_ref[:, :, j * LANES:(j + 1) * LANES]

    @pl.when(t > 0)
    def _():
        for j in range(nct):
            gbuf[:, j * tp:j * tp + CONV_PAD, :] = gbuf[:, j * tp + tt:j * tp + tt + CONV_PAD, :]

    for j in range(nct):
        ls = slice(j * LANES, (j + 1) * LANES)
        glu = za_ref[:, :, ls].astype(F32) * _sigmoid(zb_ref[:, :, ls].astype(F32))
        gbuf[:, j * tp + CONV_PAD:j * tp + CONV_PAD + tt, :] = glu
        nc_ref[:, :, ls] = gbuf[:, j * tp + tt:j * tp + tt + CONV_PAD, :]

    w = [w_ref[k] for k in range(kw)]
    bias = b_ref[...]
    nblk = tt // CONV_STEPS

    def body(i, carry):
        n = i // nblk
        t0 = (i % nblk) * CONV_STEPS
        acc = [bias] * CONV_STEPS
        for m in range(CONV_STEPS + kw - 1):
            g = gbuf[n, pl.ds(t0 + lead + m, SUBLANES, stride=tp), :]
            for s in range(max(0, m - (kw - 1)), min(CONV_STEPS - 1, m) + 1):
                acc[s] = acc[s] + g * w[m - s]
        for s in range(CONV_STEPS):
            dwbuf[n, pl.ds(t0 + s, SUBLANES, stride=tp2), :] = acc[s]
        return carry

    lax.fori_loop(0, nb * nblk, body, 0)

    slabs = [dwbuf[:, j * tp2:j * tp2 + tt, :] for j in range(nct)]
    mu = jnp.sum(functools.reduce(jnp.add, slabs), axis=-1, keepdims=True) * (1.0 / c)
    cen = [d - mu for d in slabs]
    var = jnp.sum(functools.reduce(jnp.add, [x * x for x in cen]), axis=-1, keepdims=True) * (1.0 / c)
    inv = lax.rsqrt(var + EPS)
    for j in range(nct):
        ls = slice(j * LANES, (j + 1) * LANES)
        y = cen[j] * inv * lg_ref[:, ls] + lb_ref[:, ls]
        a_ref[:, :, ls] = (y * _sigmoid(y)).astype(a_ref.dtype)


def _conv_branch(z, ctx_pad, conv_w, conv_b, ln_g, ln_b, nb, tt):
    bsz, t, _ = z.shape
    kw, c = conv_w.shape
    nct = c // LANES
    assert nct == SUBLANES and tt % CONV_STEPS == 0
    tp = _slab_rows(CONV_PAD + tt)
    tp2 = _slab_rows(tt)
    return pl.pallas_call(
        _conv_kernel,
        grid=(bsz // nb, t // tt),
        in_specs=[pl.BlockSpec((nb, tt, c), lambda i, s: (i, s, 0)),
                  pl.BlockSpec((nb, tt, c), lambda i, s: (i, s, 1)),
                  pl.BlockSpec((nb, CONV_PAD, c), lambda i, s: (i, 0, 0)),
                  pl.BlockSpec((kw, nct, LANES), lambda i, s: (0, 0, 0)),
                  pl.BlockSpec((nct, LANES), lambda i, s: (0, 0)),
                  pl.BlockSpec((1, c), lambda i, s: (0, 0)),
                  pl.BlockSpec((1, c), lambda i, s: (0, 0))],
        out_specs=[pl.BlockSpec((nb, tt, c), lambda i, s: (i, s, 0)),
                   pl.BlockSpec((nb, CONV_PAD, c), lambda i, s: (i, 0, 0))],
        out_shape=[jax.ShapeDtypeStruct((bsz, t, c), BF16),
                   jax.ShapeDtypeStruct((bsz, CONV_PAD, c), F32)],
        scratch_shapes=[pltpu.VMEM((nb, nct * tp, LANES), F32),
                        pltpu.VMEM((nb, nct * tp2, LANES), F32)],
        compiler_params=_params(2),
        name="conv_branch",
    )(z, z, ctx_pad, conv_w.reshape(kw, nct, LANES), conv_b.reshape(nct, LANES), ln_g, ln_b)


HEAD_BLOCK = 256
ATTN_UNROLL = 4


def _split_dot(x, w, split):
    hi = x.astype(BF16)
    out = jnp.dot(hi, w, preferred_element_type=F32)
    if split:
        lo = (x - hi.astype(F32)).astype(BF16)
        out = out + jnp.dot(lo, w, preferred_element_type=F32)
    return out


def _heads_norm_rope(x, g, cos, sin, mean_w, swap_w, split):
    ms = _split_dot(x * x, mean_w, split)
    y = x * lax.rsqrt(ms + EPS) * g
    return y * cos + _split_dot(y, swap_w, split) * sin


def _attn_kernel(sink_ref, q_ref, k_ref, v_ref, kc_ref, vc_ref, cos_ref, sin_ref, qg_ref, kg_ref, mw_ref, sw_ref,
                 o_ref, nk_ref, nv_ref, kf, vf, kx, vx, qs, s_scr, p_scr, *, mask_prefix):
    tq = q_ref.shape[1]
    kvw = k_ref.shape[2]
    n_kv = kvw // HEAD_DIM
    t = pl.program_id(1)
    reps = HEAD_BLOCK // LANES
    cos = jnp.concatenate([cos_ref[...]] * reps, axis=1)
    sin = jnp.concatenate([sin_ref[...]] * reps, axis=1)

    @pl.when(t == 0)
    def _():
        kf[0:WINDOW, :] = kc_ref[0]
        vf[0:WINDOW, :] = vc_ref[0]

    @pl.when(t > 0)
    def _():
        kf[0:WINDOW, :] = kf[tq:tq + WINDOW, :]
        vf[0:WINDOW, :] = vf[tq:tq + WINDOW, :]

    for j in range(kvw // HEAD_BLOCK):
        ls = slice(j * HEAD_BLOCK, (j + 1) * HEAD_BLOCK)
        kf[WINDOW:WINDOW + tq, ls] = _heads_norm_rope(k_ref[0, :, ls].astype(F32), kg_ref[...], cos, sin,
                                                     mw_ref[...], sw_ref[...], True)
    vf[WINDOW:WINDOW + tq, :] = v_ref[0].astype(F32)
    nk_ref[0] = kf[tq:tq + WINDOW, :]
    nv_ref[0] = vf[tq:tq + WINDOW, :]

    n_buf = WINDOW + tq
    lane = lax.broadcasted_iota(jnp.int32, (n_buf, LANES), 1)
    first = lane < HEAD_DIM
    for src, dst, fill in ((kf, kx, 0.0), (vf, vx, 1.0)):
        for j in range(kvw // LANES):
            tile = src[:, j * LANES:(j + 1) * LANES]
            swapped = pltpu.roll(tile, HEAD_DIM, 1)
            variants = (jnp.where(first, tile, fill), jnp.where(first, fill, swapped),
                        jnp.where(first, swapped, fill), jnp.where(first, fill, tile))
            for m, val in enumerate(variants):
                dst[0:n_buf, (4 * j + m) * LANES:(4 * j + m + 1) * LANES] = val.astype(BF16)
    kx[n_buf:n_buf + CHUNK, :] = jnp.zeros((CHUNK, kx.shape[1]), BF16)

    log2e = 1.4426950408889634
    scale = HEAD_DIM ** -0.5 * log2e
    for j in range(q_ref.shape[2] // HEAD_BLOCK):
        ls = slice(j * HEAD_BLOCK, (j + 1) * HEAD_BLOCK)
        qn = _heads_norm_rope(q_ref[0, :, ls].astype(F32), qg_ref[...], cos, sin,
                              mw_ref[...], sw_ref[...], False)
        qs[:, ls] = (qn * scale).astype(BF16)

    n_keys = WINDOW + CHUNK
    n_cols = n_keys + CHUNK
    n_chunks = tq // CHUNK
    rows = 2 * CHUNK
    dn = (((1,), (1,)), ((), ()))
    lane2 = lax.broadcasted_iota(jnp.int32, (rows, LANES), 1)
    row2 = lax.broadcasted_iota(jnp.int32, (rows, LANES), 0)
    first2 = lane2 < HEAD_DIM

    def score_body(c, carry):
        r0 = pl.multiple_of(c * CHUNK, CHUNK)
        for h in range(n_kv):
            q2 = jnp.concatenate([qs[pl.ds(r0, CHUNK), (2 * h) * LANES:(2 * h + 1) * LANES],
                                  qs[pl.ds(r0, CHUNK), (2 * h + 1) * LANES:(2 * h + 2) * LANES]], axis=0)
            for v in range(2):
                kt = kx[pl.ds(r0, n_cols), (2 * h + v) * LANES:(2 * h + v + 1) * LANES]
                s = lax.dot_general(q2, kt, dn, preferred_element_type=F32)
                sink = jnp.where(row2 < CHUNK, sink_ref[4 * h + v], sink_ref[4 * h + 2 + v]) * log2e
                pad = jnp.where(lane2 == n_keys - LANES, sink, NEG_INF)
                blk = 2 * (c * n_kv + h) + v
                s_scr[blk, :, 0:LANES] = s[:, 0:LANES]
                s_scr[blk, :, LANES:n_cols] = jnp.where(first2, s[:, LANES:n_cols], pad)
        return carry

    lax.fori_loop(0, n_chunks, score_body, 0, unroll=min(ATTN_UNROLL, n_chunks))

    if mask_prefix:
        for c in range(min(2, n_chunks)):
            g = t * n_chunks + c
            blocks = slice(2 * c * n_kv, 2 * (c + 1) * n_kv)

            @pl.when(g == 0)
            def _():
                s_scr[blocks, :, 0:LANES] = jnp.full((2 * n_kv, rows, LANES), NEG_INF, F32)

            @pl.when(g == 1)
            def _():
                s_scr[blocks, :, 0:LANES] = jnp.where(first2[None], NEG_INF, s_scr[blocks, :, 0:LANES])

    s_all = s_scr[...]
    m_all = jnp.max(s_all, axis=-1, keepdims=True)
    p_scr[...] = jnp.exp2(s_all - m_all).astype(BF16)

    e_row = lax.broadcasted_iota(jnp.int32, (CHUNK, LANES), 0) == 0
    e_lane = lax.broadcasted_iota(jnp.int32, (CHUNK, LANES), 1) < HEAD_DIM
    e_lo = jnp.where(jnp.logical_and(e_row, jnp.logical_not(e_lane)), 1.0, 0.0).astype(BF16)
    e_hi = jnp.where(jnp.logical_and(e_row, e_lane), 1.0, 0.0).astype(BF16)

    def out_body(c, carry):
        r0 = pl.multiple_of(c * CHUNK, CHUNK)
        for h in range(n_kv):
            blk = 2 * (c * n_kv + h)
            v_lo = jnp.concatenate([vx[pl.ds(r0, n_keys), (2 * h) * LANES:(2 * h + 1) * LANES], e_lo], axis=0)
            v_hi = jnp.concatenate([vx[pl.ds(r0, n_keys), (2 * h + 1) * LANES:(2 * h + 2) * LANES], e_hi], axis=0)
            o_lo = jnp.dot(p_scr[blk], v_lo, preferred_element_type=F32)
            o_hi = jnp.dot(p_scr[blk + 1], v_hi, preferred_element_type=F32)
            num = jnp.where(first2, o_lo, o_hi)
            den = pltpu.roll(jnp.where(first2, o_hi, o_lo), HEAD_DIM, 1)
            o2 = (num / den).astype(o_ref.dtype)
            o_ref[0, pl.ds(r0, CHUNK), (2 * h) * LANES:(2 * h + 1) * LANES] = o2[0:CHUNK]
            o_ref[0, pl.ds(r0, CHUNK), (2 * h + 1) * LANES:(2 * h + 2) * LANES] = o2[CHUNK:2 * CHUNK]
        return carry

    lax.fori_loop(0, n_chunks, out_body, 0, unroll=min(ATTN_UNROLL, n_chunks))


def _attention(z, k_cache, v_cache, cos, sin, q_g, k_g, sinks, q_col, aw, kvw, tq, mask_prefix):
    bsz, t, _ = z.shape
    assert q_col % aw == 0 and (q_col + aw) % kvw == 0 and aw // kvw == 4 and kvw % LANES == 0
    k_blk = (q_col + aw) // kvw
    n_blocks = 2 * (tq // CHUNK) * (kvw // HEAD_DIM)
    assert kvw % HEAD_BLOCK == 0 and aw % HEAD_BLOCK == 0
    idx = jnp.arange(HEAD_BLOCK)
    mean_w = jnp.where(idx[:, None] // HEAD_DIM == idx[None, :] // HEAD_DIM, 1.0 / HEAD_DIM, 0.0).astype(BF16)
    swap_w = (idx[:, None] == (idx[None, :] ^ (HEAD_DIM // 2))).astype(BF16)
    return pl.pallas_call(
        functools.partial(_attn_kernel, mask_prefix=mask_prefix),
        grid=(bsz, t // tq),
        in_specs=[pl.BlockSpec(memory_space=pltpu.SMEM),
                  pl.BlockSpec((1, tq, aw), lambda i, s: (i, s, q_col // aw)),
                  pl.BlockSpec((1, tq, kvw), lambda i, s: (i, s, k_blk)),
                  pl.BlockSpec((1, tq, kvw), lambda i, s: (i, s, k_blk + 1)),
                  pl.BlockSpec((1, WINDOW, kvw), lambda i, s: (i, 0, 0)),
                  pl.BlockSpec((1, WINDOW, kvw), lambda i, s: (i, 0, 0)),
                  pl.BlockSpec((tq, LANES), lambda i, s: (s, 0)),
                  pl.BlockSpec((tq, LANES), lambda i, s: (s, 0)),
                  pl.BlockSpec((1, HEAD_BLOCK), lambda i, s: (0, 0)),
                  pl.BlockSpec((1, HEAD_BLOCK), lambda i, s: (0, 0)),
                  pl.BlockSpec((HEAD_BLOCK, HEAD_BLOCK), lambda i, s: (0, 0)),
                  pl.BlockSpec((HEAD_BLOCK, HEAD_BLOCK), lambda i, s: (0, 0))],
        out_specs=[pl.BlockSpec((1, tq, aw), lambda i, s: (i, s, 0)),
                   pl.BlockSpec((1, WINDOW, kvw), lambda i, s: (i, 0, 0)),
                   pl.BlockSpec((1, WINDOW, kvw), lambda i, s: (i, 0, 0))],
        scratch_shapes=[pltpu.VMEM((WINDOW + tq, kvw), F32),
                        pltpu.VMEM((WINDOW + tq, kvw), F32),
                        pltpu.VMEM((WINDOW + tq + CHUNK, 4 * kvw), BF16),
                        pltpu.VMEM((WINDOW + tq, 4 * kvw), BF16),
                        pltpu.VMEM((tq, aw), BF16),
                        pltpu.VMEM((n_blocks, 2 * CHUNK, WINDOW + 2 * CHUNK), F32),
                        pltpu.VMEM((n_blocks, 2 * CHUNK, WINDOW + 2 * CHUNK), BF16)],
        out_shape=[jax.ShapeDtypeStruct((bsz, t, aw), BF16),
                   jax.ShapeDtypeStruct((bsz, WINDOW, kvw), F32),
                   jax.ShapeDtypeStruct((bsz, WINDOW, kvw), F32)],
        compiler_params=_params(2),
        name="attention",
    )(sinks, z, z, z, k_cache, v_cache, cos, sin, q_g, k_g, mean_w, swap_w)


def _merge_kernel(a_ref, o_ref, gc_ref, ga_ref, wc_ref, wa_ref, wo_ref, x_ref, g1_ref, y_ref, mg):
    nb, tt, d = x_ref.shape
    tn = wc_ref.shape[1]
    n = pl.program_id(2)
    a = a_ref[...].reshape(nb * tt, -1)
    o = o_ref[...].reshape(nb * tt, -1)
    yc = jnp.dot(a, wc_ref[...], preferred_element_type=F32)
    ya = jnp.dot(o, wa_ref[...], preferred_element_type=F32)
    gc = gc_ref[...].astype(F32).reshape(nb * tt, tn)
    ga = ga_ref[...].astype(F32).reshape(nb * tt, tn)
    merged = _sigmoid(gc) * yc + _sigmoid(ga) * ya
    mg[:, pl.ds(pl.multiple_of(n * tn, tn), tn)] = merged.astype(BF16)

    @pl.when(n == pl.num_programs(2) - 1)
    def _():
        proj = jnp.dot(mg[...], wo_ref[...], preferred_element_type=F32).reshape(nb, tt, d)
        y_ref[...] = x_ref[...] + g1_ref[...] * proj


def _merge(a, o, z, gc_col, ga_col, wc, wa, wo, x, mod3, row0, nb, tt, tn):
    bsz, t, d = x.shape
    c = a.shape[2]
    aw = o.shape[2]
    assert gc_col % tn == 0 and ga_col % tn == 0 and d % tn == 0
    return pl.pallas_call(
        _merge_kernel,
        grid=(bsz // nb, t // tt, d // tn),
        in_specs=[pl.BlockSpec((nb, tt, c), lambda i, s, n: (i, s, 0)),
                  pl.BlockSpec((nb, tt, aw), lambda i, s, n: (i, s, 0)),
                  pl.BlockSpec((nb, tt, tn), lambda i, s, n: (i, s, gc_col // tn + n)),
                  pl.BlockSpec((nb, tt, tn), lambda i, s, n: (i, s, ga_col // tn + n)),
                  pl.BlockSpec((c, tn), lambda i, s, n: (0, n)),
                  pl.BlockSpec((aw, tn), lambda i, s, n: (0, n)),
                  pl.BlockSpec((d, d), lambda i, s, n: (0, 0)),
                  pl.BlockSpec((nb, tt, d), lambda i, s, n: (i, s, 0)),
                  pl.BlockSpec((nb, 1, d), lambda i, s, n: (row0 // nb + i, 0, 2))],
        out_specs=pl.BlockSpec((nb, tt, d), lambda i, s, n: (i, s, 0)),
        out_shape=jax.ShapeDtypeStruct((bsz, t, d), F32),
        scratch_shapes=[pltpu.VMEM((nb * tt, d), BF16)],
        compiler_params=_params(3),
        name="merge_out",
    )(a, o, z, z, wc, wa, wo, x, mod3)


FFN_PAD = SUBLANES
FFN_SUB = 256


def _ffn_kernel(x_ref, sc_ref, sh_ref, g2_ref, ng_ref, wg_ref, wv_ref, cw_ref, cb_ref, wd_ref, ctx_ref,
                y_ref, nf_ref, h_ref, ubuf, vbuf, halo):
    nb, tt, d = x_ref.shape
    tf = wg_ref.shape[1]
    kw = cw_ref.shape[0]
    t = pl.program_id(1)
    f = pl.program_id(2)

    @pl.when(f == 0)
    def _():
        h = _adaln(x_ref[...], ng_ref[...], sc_ref[...], sh_ref[...])
        h_ref[...] = h.reshape(nb * tt, d).astype(BF16)
        y_ref[...] = jnp.zeros(y_ref.shape, F32)

    @pl.when(t == 0)
    def _():
        ubuf[:, 0:FFN_PAD, :] = ctx_ref[...]

    @pl.when(t > 0)
    def _():
        ubuf[:, 0:FFN_PAD, :] = halo[f]

    h = h_ref[...]
    part = None
    for c0 in range(0, tf, FFN_SUB):
        cs = slice(c0, c0 + FFN_SUB)
        ug = jnp.dot(h, wg_ref[:, cs], preferred_element_type=F32)
        ubuf[:, FFN_PAD:FFN_PAD + tt, cs] = ug.reshape(nb, tt, FFN_SUB)
        vbuf[:, cs] = jnp.dot(h, wv_ref[:, cs], preferred_element_type=F32)
    for c0 in range(0, tf, FFN_SUB):
        cs = slice(c0, c0 + FFN_SUB)
        uv = vbuf[:, cs]
        conv = jnp.broadcast_to(cb_ref[:, cs].reshape(1, 1, FFN_SUB), (nb, tt, FFN_SUB))
        for k in range(kw):
            lead = FFN_PAD - (kw - 1) + k
            conv = conv + ubuf[:, lead:lead + tt, cs] * cw_ref[k:k + 1, cs].reshape(1, 1, FFN_SUB)
        act = (conv * _sigmoid(conv)).reshape(nb * tt, FFN_SUB) * uv
        contrib = jnp.dot(act.astype(BF16), wd_ref[cs, :], preferred_element_type=F32)
        part = contrib if part is None else part + contrib
    part = part.reshape(nb, tt, d)

    tail = ubuf[:, tt:tt + FFN_PAD, :]
    halo[f] = tail
    nf_ref[:, :, pl.ds(pl.multiple_of(f * tf, tf), tf)] = tail

    y_ref[...] += part

    @pl.when(f == pl.num_programs(2) - 1)
    def _():
        y_ref[...] = x_ref[...] + g2_ref[...] * y_ref[...]


def _ffn(x, mod3, row0, norm_g, w_up, conv_w, conv_b, w_down, ctx_pad, nb, tt, tf):
    bsz, t, d = x.shape
    dff = w_down.shape[0]
    kw = conv_w.shape[0]
    nf = dff // tf
    assert dff % tf == 0 and nf >= 2 and tf % FFN_SUB == 0
    return pl.pallas_call(
        _ffn_kernel,
        grid=(bsz // nb, t // tt, nf),
        in_specs=[pl.BlockSpec((nb, tt, d), lambda i, s, f: (i, s, 0), pipeline_mode=pl.Buffered(1)),
                  pl.BlockSpec((nb, 1, d), lambda i, s, f: (row0 // nb + i, 0, 4)),
                  pl.BlockSpec((nb, 1, d), lambda i, s, f: (row0 // nb + i, 0, 3)),
                  pl.BlockSpec((nb, 1, d), lambda i, s, f: (row0 // nb + i, 0, 5)),
                  pl.BlockSpec((1, d), lambda i, s, f: (0, 0)),
                  pl.BlockSpec((d, tf), lambda i, s, f: (0, f)),
                  pl.BlockSpec((d, tf), lambda i, s, f: (0, nf + f)),
                  pl.BlockSpec((kw, tf), lambda i, s, f: (0, f)),
                  pl.BlockSpec((1, tf), lambda i, s, f: (0, f)),
                  pl.BlockSpec((tf, d), lambda i, s, f: (f, 0)),
                  pl.BlockSpec((nb, FFN_PAD, tf), lambda i, s, f: (i, 0, f))],
        out_specs=[pl.BlockSpec((nb, tt, d), lambda i, s, f: (i, s, 0)),
                   pl.BlockSpec((nb, FFN_PAD, dff), lambda i, s, f: (i, 0, 0))],
        out_shape=[jax.ShapeDtypeStruct((bsz, t, d), F32),
                   jax.ShapeDtypeStruct((bsz, FFN_PAD, dff), F32)],
        scratch_shapes=[pltpu.VMEM((nb * tt, d), BF16),
                        pltpu.VMEM((nb, FFN_PAD + tt, tf), F32),
                        pltpu.VMEM((nb * tt, tf), F32),
                        pltpu.VMEM((nf, nb, FFN_PAD, tf), F32)],
        compiler_params=_params(3),
        name="conv_ffn",
    )(x, mod3, mod3, mod3, norm_g, w_up, w_up, conv_w, conv_b, w_down, ctx_pad)


def _rope_tables(pos):
    half = HEAD_DIM // 2
    inv_freq = 1.0 / (ROPE_THETA ** (jnp.arange(half, dtype=F32) / half))
    ang = pos.astype(F32)[:, None] * inv_freq[None, :]
    cos = jnp.cos(ang)
    sin = jnp.sin(ang)
    reps = LANES // HEAD_DIM
    return (jnp.tile(jnp.concatenate([cos, cos], axis=-1), (1, reps)),
            jnp.tile(jnp.concatenate([-sin, sin], axis=-1), (1, reps)))


def _front_pad(ctx, rows):
    return jnp.pad(ctx, ((0, 0), (rows - ctx.shape[1], 0), (0, 0)))


def _layer(x, mod3, row0, pos, conv_ctx, k_cache, v_cache, ffn_ctx, mask_prefix, p, tiles):
    bsz, t, d = x.shape
    c = p["conv_w"].shape[1]
    kvw = k_cache.shape[2] * k_cache.shape[3]
    aw = p["attn_o_w"].shape[0]
    nb, tt, tq = tiles["nb"], tiles["tt"], tiles["tq"]
    assert row0 % nb == 0 and bsz % nb == 0 and t % tt == 0 and t % tq == 0
    assert row0 % tiles["nb_merge"] == 0 and bsz % tiles["nb_merge"] == 0

    z = _in_proj(x, mod3, row0, p["norm1_g"], p["w_in"], p["b_in"], nb, tiles["tt_big"], tiles["tn_in"])
    a, nc = _conv_branch(z, _front_pad(conv_ctx, CONV_PAD), p["conv_w"], p["conv_b"], p["ln_g"], p["ln_b"],
                         nb, tt)
    cos, sin = _rope_tables(pos)
    reps = HEAD_BLOCK // HEAD_DIM
    o, nk, nv = _attention(z, k_cache.reshape(bsz, -1, kvw), v_cache.reshape(bsz, -1, kvw), cos, sin,
                           jnp.tile(p["q_norm_g"], reps)[None, :], jnp.tile(p["k_norm_g"], reps)[None, :],
                           p["sinks"], 2 * c, aw, kvw, tq, mask_prefix)
    gc_col = 2 * c + aw + 2 * kvw
    x1 = _merge(a, o, z, gc_col, gc_col + d, p["conv_out_w"], p["attn_o_w"], p["w_out"], x, mod3, row0,
                tiles["nb_merge"], tt, tiles["tn_merge"])
    y, nf = _ffn(x1, mod3, row0, p["norm2_g"], p["ffn_up_w"], p["ffn_conv_w"], p["ffn_conv_b"],
                 p["ffn_down_w"], _front_pad(ffn_ctx, FFN_PAD), nb, tiles["tt_big"], tiles["tf"])
    kw = p["conv_w"].shape[0]
    fkw = p["ffn_conv_w"].shape[0]
    return (y, nc[:, CONV_PAD - (kw - 1):], nk.reshape(k_cache.shape), nv.reshape(v_cache.shape),
            nf[:, FFN_PAD - (fkw - 1):])


def kernel(x_prompt, x_sample, c_prompt, c_sample, cache_conv, cache_k, cache_v, cache_ffn_conv, mod_w, mod_b, norm1_g, w_in, b_in, conv_w, conv_b, ln_g, ln_b, conv_out_w, q_norm_g, k_norm_g, sinks, attn_o_w, w_out, norm2_g, ffn_up_w, ffn_conv_w, ffn_conv_b, ffn_down_w):
    depth = mod_w.shape[0]
    bp, tp, d = x_prompt.shape
    bs, ts, _ = x_sample.shape
    pos_p = jnp.arange(tp)
    pos_s = PAST_LEN + jnp.arange(ts)
    yp, ys = x_prompt, x_sample
    outs = [[] for _ in range(8)]
    tiles_p = dict(nb=1, nb_merge=1, tt=min(512, tp), tt_big=min(1024, tp), tq=min(512, tp), tn_in=1536, tn_merge=512,
                   tf=512)
    tiles_s = dict(nb=min(16, bs), nb_merge=min(8, bs), tt=ts, tt_big=ts, tq=ts, tn_in=1536, tn_merge=512,
                   tf=512)
    for l in range(depth):
        p = dict(norm1_g=norm1_g[l][None, :], w_in=w_in[l].astype(BF16), b_in=b_in[l][None, :],
                 conv_w=conv_w[l], conv_b=conv_b[l][None, :], ln_g=ln_g[l][None, :], ln_b=ln_b[l][None, :],
                 conv_out_w=conv_out_w[l].astype(BF16), q_norm_g=q_norm_g[l], k_norm_g=k_norm_g[l],
                 sinks=sinks[l], attn_o_w=attn_o_w[l].astype(BF16), w_out=w_out[l].astype(BF16),
                 norm2_g=norm2_g[l][None, :], ffn_up_w=ffn_up_w[l].astype(BF16), ffn_conv_w=ffn_conv_w[l],
                 ffn_conv_b=ffn_conv_b[l][None, :], ffn_down_w=ffn_down_w[l].astype(BF16))
        c_all = jnp.concatenate([c_prompt, c_sample], axis=0)
        mod3 = _mod(c_all, mod_w[l], mod_b[l][None, :])[:, None, :]
        zeros_conv = jnp.zeros((bp,) + cache_conv.shape[2:], F32)
        zeros_kv = jnp.zeros((bp,) + cache_k.shape[2:], F32)
        zeros_ffn = jnp.zeros((bp,) + cache_ffn_conv.shape[2:], F32)
        yp, nc_p, nk_p, nv_p, nf_p = _layer(yp, mod3, 0, pos_p, zeros_conv, zeros_kv, zeros_kv, zeros_ffn,
                                            True, p, tiles_p)
        ys, nc_s, nk_s, nv_s, nf_s = _layer(ys, mod3, bp, pos_s, cache_conv[l], cache_k[l], cache_v[l],
                                            cache_ffn_conv[l], False, p, tiles_s)
        for lst, val in zip(outs, (nc_p, nc_s, nk_p, nk_s, nv_p, nv_s, nf_p, nf_s)):
            lst.append(val)
    return (yp, ys) + tuple(jnp.stack(o) for o in outs)
```

```python
import functools

import jax
import jax.numpy as jnp
from jax import lax
from jax.experimental import pallas as pl
from jax.experimental.pallas import tpu as pltpu

CHUNK = 64
HEAD_DIM = 64
WINDOW = 128
PAST_LEN = 1024
ROPE_THETA = 10000.0
EPS = 1e-6
NEG_INF = -1e30
LANES = 128
SUBLANES = 8
VMEM_LIMIT_BYTES = 60 * 1024 * 1024

F32 = jnp.float32
BF16 = jnp.bfloat16


def _params(n_axes):
    return pltpu.CompilerParams(dimension_semantics=("arbitrary",) * n_axes,
                                vmem_limit_bytes=VMEM_LIMIT_BYTES)


def _sigmoid(x):
    return 1.0 / (1.0 + jnp.exp(-x))


def _mod_kernel(c_ref, w_ref, b_ref, o_ref):
    c = c_ref[...]
    a = (c * _sigmoid(c)).astype(BF16)
    o_ref[...] = jnp.dot(a, w_ref[...].astype(BF16), preferred_element_type=F32) + b_ref[...]


def _mod(c_all, mod_w, mod_b, tn=1024):
    m, d = c_all.shape
    n = mod_w.shape[1]
    return pl.pallas_call(
        _mod_kernel,
        grid=(n // tn,),
        in_specs=[pl.BlockSpec((m, d), lambda j: (0, 0)),
                  pl.BlockSpec((d, tn), lambda j: (0, j)),
                  pl.BlockSpec((1, tn), lambda j: (0, j))],
        out_specs=pl.BlockSpec((m, tn), lambda j: (0, j)),
        out_shape=jax.ShapeDtypeStruct((m, n), F32),
        compiler_params=_params(1),
        name="mod",
    )(c_all, mod_w, mod_b)


PROLOGUE_PARTS = 4


def _row_parts(nb, tt, parts):
    if nb >= parts:
        step = nb // parts
        return [(slice(i * step, (i + 1) * step), slice(0, tt)) for i in range(parts)]
    step = tt // parts
    return [(slice(b, b + 1), slice(i * step, (i + 1) * step)) for b in range(nb) for i in range(parts)]


def _adaln(x, g, sc, sh):
    ms = jnp.mean(x * x, axis=-1, keepdims=True)
    y = x * lax.rsqrt(ms + EPS) * g
    return y * (1.0 + sc) + sh


def _in_kernel(x_ref, sc_ref, sh_ref, g_ref, w_ref, b_ref, o_ref, h_ref):
    nb, tt, d = x_ref.shape
    j = pl.program_id(2)

    @pl.when(j == 0)
    def _():
        for bs, ts in _row_parts(nb, tt, PROLOGUE_PARTS):
            rows = (bs.stop - bs.start) * (ts.stop - ts.start)
            r0 = bs.start * tt + ts.start
            h = _adaln(x_ref[bs, ts, :], g_ref[...], sc_ref[bs], sh_ref[bs]).reshape(rows, d).astype(BF16)
            h_ref[r0:r0 + rows, :] = h
            z = jnp.dot(h, w_ref[...], preferred_element_type=F32) + b_ref[...]
            o_ref[bs, ts, :] = z.reshape(bs.stop - bs.start, ts.stop - ts.start, -1).astype(o_ref.dtype)

    @pl.when(j > 0)
    def _():
        z = jnp.dot(h_ref[...], w_ref[...], preferred_element_type=F32) + b_ref[...]
        o_ref[...] = z.reshape(nb, tt, -1).astype(o_ref.dtype)


def _in_proj(x, mod3, row0, norm_g, w, b, nb, tt, tn):
    bsz, t, d = x.shape
    n = w.shape[1]
    return pl.pallas_call(
        _in_kernel,
        grid=(bsz // nb, t // tt, n // tn),
        in_specs=[pl.BlockSpec((nb, tt, d), lambda i, s, j: (i, s, 0)),
                  pl.BlockSpec((nb, 1, d), lambda i, s, j: (row0 // nb + i, 0, 1)),
                  pl.BlockSpec((nb, 1, d), lambda i, s, j: (row0 // nb + i, 0, 0)),
                  pl.BlockSpec((1, d), lambda i, s, j: (0, 0)),
                  pl.BlockSpec((d, tn), lambda i, s, j: (0, j)),
                  pl.BlockSpec((1, tn), lambda i, s, j: (0, j))],
        out_specs=pl.BlockSpec((nb, tt, tn), lambda i, s, j: (i, s, j)),
        out_shape=jax.ShapeDtypeStruct((bsz, t, n), BF16),
        scratch_shapes=[pltpu.VMEM((nb * tt, d), BF16)],
        compiler_params=_params(3),
        name="in_proj",
    )(x, mod3, mod3, norm_g, w, b)


CONV_PAD = 32
CONV_STEPS = 16


def _slab_rows(rows):
    pitch = -(-rows // SUBLANES) * SUBLANES
    return pitch if (pitch // SUBLANES) % 2 else pitch + SUBLANES


def _conv_kernel(za_ref, zb_ref, ctx_ref, w_ref, b_ref, lg_ref, lb_ref, a_ref, nc_ref, gbuf, dwbuf):
    nb, tt, c = za_ref.shape
    nct = c // LANES
    kw = w_ref.shape[0]
    tp = gbuf.shape[1] // nct
    tp2 = dwbuf.shape[1] // nct
    lead = CONV_PAD - (kw - 1)
    t = pl.program_id(1)

    @pl.when(t == 0)
    def _():
        for j in range(nct):
            gbuf[:, j * tp:j * tp + CONV_PAD, :] = ctx_ref[:, :, j * LANES:(j + 1) * LANES]

    @pl.when(t > 0)
    def _():
        for j in range(nct):
            gbuf[:, j * tp:j * tp + CONV_PAD, :] = gbuf[:, j * tp + tt:j * tp + tt + CONV_PAD, :]

    for j in range(nct):
        ls = slice(j * LANES, (j + 1) * LANES)
        glu = za_ref[:, :, ls].astype(F32) * _sigmoid(zb_ref[:, :, ls].astype(F32))
        gbuf[:, j * tp + CONV_PAD:j * tp + CONV_PAD + tt, :] = glu
        nc_ref[:, :, ls] = gbuf[:, j * tp + tt:j * tp + tt + CONV_PAD, :]

    w = [w_ref[k] for k in range(kw)]
    bias = b_ref[...]
    nblk = tt // CONV_STEPS

    def body(i, carry):
        n = i // nblk
        t0 = (i % nblk) * CONV_STEPS
        acc = [bias] * CONV_STEPS
        for m in range(CONV_STEPS + kw - 1):
            g = gbuf[n, pl.ds(t0 + lead + m, SUBLANES, stride=tp), :]
            for s in range(max(0, m - (kw - 1)), min(CONV_STEPS - 1, m) + 1):
                acc[s] = acc[s] + g * w[m - s]
        for s in range(CONV_STEPS):
            dwbuf[n, pl.ds(t0 + s, SUBLANES, stride=tp2), :] = acc[s]
        return carry

    lax.fori_loop(0, nb * nblk, body, 0)

    slabs = [dwbuf[:, j * tp2:j * tp2 + tt, :] for j in range(nct)]
    mu = jnp.sum(functools.reduce(jnp.add, slabs), axis=-1, keepdims=True) * (1.0 / c)
    cen = [d - mu for d in slabs]
    var = jnp.sum(functools.reduce(jnp.add, [x * x for x in cen]), axis=-1, keepdims=True) * (1.0 / c)
    inv = lax.rsqrt(var + EPS)
    for j in range(nct):
        ls = slice(j * LANES, (j + 1) * LANES)
        y = cen[j] * inv * lg_ref[:, ls] + lb_ref[:, ls]
        a_ref[:, :, ls] = (y * _sigmoid(y)).astype(a_ref.dtype)


def _conv_branch(z, ctx_pad, conv_w, conv_b, ln_g, ln_b, nb, tt):
    bsz, t, _ = z.shape
    kw, c = conv_w.shape
    nct = c // LANES
    assert nct == SUBLANES and tt % CONV_STEPS == 0
    tp = _slab_rows(CONV_PAD + tt)
    tp2 = _slab_rows(tt)
    return pl.pallas_call(
        _conv_kernel,
        grid=(bsz // nb, t // tt),
        in_specs=[pl.BlockSpec((nb, tt, c), lambda i, s: (i, s, 0)),
                  pl.BlockSpec((nb, tt, c), lambda i, s: (i, s, 1)),
                  pl.BlockSpec((nb, CONV_PAD, c), lambda i, s: (i, 0, 0)),
                  pl.BlockSpec((kw, nct, LANES), lambda i, s: (0, 0, 0)),
                  pl.BlockSpec((nct, LANES), lambda i, s: (0, 0)),
                  pl.BlockSpec((1, c), lambda i, s: (0, 0)),
                  pl.BlockSpec((1, c), lambda i, s: (0, 0))],
        out_specs=[pl.BlockSpec((nb, tt, c), lambda i, s: (i, s, 0)),
                   pl.BlockSpec((nb, CONV_PAD, c), lambda i, s: (i, 0, 0))],
        out_shape=[jax.ShapeDtypeStruct((bsz, t, c), BF16),
                   jax.ShapeDtypeStruct((bsz, CONV_PAD, c), F32)],
        scratch_shapes=[pltpu.VMEM((nb, nct * tp, LANES), F32),
                        pltpu.VMEM((nb, nct * tp2, LANES), F32)],
        compiler_params=_params(2),
        name="conv_branch",
    )(z, z, ctx_pad, conv_w.reshape(kw, nct, LANES), conv_b.reshape(nct, LANES), ln_g, ln_b)


HEAD_BLOCK = 256
ATTN_UNROLL = 4


def _split_dot(x, w, split):
    hi = x.astype(BF16)
    out = jnp.dot(hi, w, preferred_element_type=F32)
    if split:
        lo = (x - hi.astype(F32)).astype(BF16)
        out = out + jnp.dot(lo, w, preferred_element_type=F32)
    return out


def _heads_norm_rope(x, g, cos, sin, mean_w, swap_w, split):
    ms = _split_dot(x * x, mean_w, split)
    y = x * lax.rsqrt(ms + EPS) * g
    return y * cos + _split_dot(y, swap_w, split) * sin


def _attn_kernel(sink_ref, q_ref, k_ref, v_ref, kc_ref, vc_ref, cos_ref, sin_ref, qg_ref, kg_ref, mw_ref, sw_ref,
                 o_ref, nk_ref, nv_ref, kf, vf, kx, vx, qs, s_scr, p_scr, *, mask_prefix):
    tq = q_ref.shape[1]
    kvw = k_ref.shape[2]
    n_kv = kvw // HEAD_DIM
    t = pl.program_id(1)
    reps = HEAD_BLOCK // LANES
    cos = jnp.concatenate([cos_ref[...]] * reps, axis=1)
    sin = jnp.concatenate([sin_ref[...]] * reps, axis=1)

    @pl.when(t == 0)
    def _():
        kf[0:WINDOW, :] = kc_ref[0]
        vf[0:WINDOW, :] = vc_ref[0]

    @pl.when(t > 0)
    def _():
        kf[0:WINDOW, :] = kf[tq:tq + WINDOW, :]
        vf[0:WINDOW, :] = vf[tq:tq + WINDOW, :]

    for j in range(kvw // HEAD_BLOCK):
        ls = slice(j * HEAD_BLOCK, (j + 1) * HEAD_BLOCK)
        kf[WINDOW:WINDOW + tq, ls] = _heads_norm_rope(k_ref[0, :, ls].astype(F32), kg_ref[...], cos, sin,
                                                     mw_ref[...], sw_ref[...], True)
    vf[WINDOW:WINDOW + tq, :] = v_ref[0].astype(F32)
    nk_ref[0] = kf[tq:tq + WINDOW, :]
    nv_ref[0] = vf[tq:tq + WINDOW, :]

    n_buf = WINDOW + tq
    lane = lax.broadcasted_iota(jnp.int32, (n_buf, LANES), 1)
    first = lane < HEAD_DIM
    for src, dst, fill in ((kf, kx, 0.0), (vf, vx, 1.0)):
        for j in range(kvw // LANES):
            tile = src[:, j * LANES:(j + 1) * LANES]
            swapped = pltpu.roll(tile, HEAD_DIM, 1)
            variants = (jnp.where(first, tile, fill), jnp.where(first, fill, swapped),
                        jnp.where(first, swapped, fill), jnp.where(first, fill, tile))
            for m, val in enumerate(variants):
                dst[0:n_buf, (4 * j + m) * LANES:(4 * j + m + 1) * LANES] = val.astype(BF16)
    kx[n_buf:n_buf + CHUNK, :] = jnp.zeros((CHUNK, kx.shape[1]), BF16)

    log2e = 1.4426950408889634
    scale = HEAD_DIM ** -0.5 * log2e
    for j in range(q_ref.shape[2] // HEAD_BLOCK):
        ls = slice(j * HEAD_BLOCK, (j + 1) * HEAD_BLOCK)
        qn = _heads_norm_rope(q_ref[0, :, ls].astype(F32), qg_ref[...], cos, sin,
                              mw_ref[...], sw_ref[...], False)
        qs[:, ls] = (qn * scale).astype(BF16)

    n_keys = WINDOW + CHUNK
    n_cols = n_keys + CHUNK
    n_chunks = tq // CHUNK
    rows = 2 * CHUNK
    dn = (((1,), (1,)), ((), ()))
    lane2 = lax.broadcasted_iota(jnp.int32, (rows, LANES), 1)
    row2 = lax.broadcasted_iota(jnp.int32, (rows, LANES), 0)
    first2 = lane2 < HEAD_DIM

    def score_body(c, carry):
        r0 = pl.multiple_of(c * CHUNK, CHUNK)
        for h in range(n_kv):
            q2 = jnp.concatenate([qs[pl.ds(r0, CHUNK), (2 * h) * LANES:(2 * h + 1) * LANES],
                                  qs[pl.ds(r0, CHUNK), (2 * h + 1) * LANES:(2 * h + 2) * LANES]], axis=0)
            for v in range(2):
                kt = kx[pl.ds(r0, n_cols), (2 * h + v) * LANES:(2 * h + v + 1) * LANES]
                s = lax.dot_general(q2, kt, dn, preferred_element_type=F32)
                sink = jnp.where(row2 < CHUNK, sink_ref[4 * h + v], sink_ref[4 * h + 2 + v]) * log2e
                pad = jnp.where(lane2 == n_keys - LANES, sink, NEG_INF)
                blk = 2 * (c * n_kv + h) + v
                s_scr[blk, :, 0:LANES] = s[:, 0:LANES]
                s_scr[blk, :, LANES:n_cols] = jnp.where(first2, s[:, LANES:n_cols], pad)
        return carry

    lax.fori_loop(0, n_chunks, score_body, 0, unroll=min(ATTN_UNROLL, n_chunks))

    if mask_prefix:
        for c in range(min(2, n_chunks)):
            g = t * n_chunks + c
            blocks = slice(2 * c * n_kv, 2 * (c + 1) * n_kv)

            @pl.when(g == 0)
            def _():
                s_scr[blocks, :, 0:LANES] = jnp.full((2 * n_kv, rows, LANES), NEG_INF, F32)

            @pl.when(g == 1)
            def _():
                s_scr[blocks, :, 0:LANES] = jnp.where(first2[None], NEG_INF, s_scr[blocks, :, 0:LANES])

    s_all = s_scr[...]
    m_all = jnp.max(s_all, axis=-1, keepdims=True)
    p_scr[...] = jnp.exp2(s_all - m_all).astype(BF16)

    e_row = lax.broadcasted_iota(jnp.int32, (CHUNK, LANES), 0) == 0
    e_lane = lax.broadcasted_iota(jnp.int32, (CHUNK, LANES), 1) < HEAD_DIM
    e_lo = jnp.where(jnp.logical_and(e_row, jnp.logical_not(e_lane)), 1.0, 0.0).astype(BF16)
    e_hi = jnp.where(jnp.logical_and(e_row, e_lane), 1.0, 0.0).astype(BF16)

    def out_body(c, carry):
        r0 = pl.multiple_of(c * CHUNK, CHUNK)
        for h in range(n_kv):
            blk = 2 * (c * n_kv + h)
            v_lo = jnp.concatenate([vx[pl.ds(r0, n_keys), (2 * h) * LANES:(2 * h + 1) * LANES], e_lo], axis=0)
            v_hi = jnp.concatenate([vx[pl.ds(r0, n_keys), (2 * h + 1) * LANES:(2 * h + 2) * LANES], e_hi], axis=0)
            o_lo = jnp.dot(p_scr[blk], v_lo, preferred_element_type=F32)
            o_hi = jnp.dot(p_scr[blk + 1], v_hi, preferred_element_type=F32)
            num = jnp.where(first2, o_lo, o_hi)
            den = pltpu.roll(jnp.where(first2, o_hi, o_lo), HEAD_DIM, 1)
            o2 = (num / den).astype(o_ref.dtype)
            o_ref[0, pl.ds(r0, CHUNK), (2 * h) * LANES:(2 * h + 1) * LANES] = o2[0:CHUNK]
            o_ref[0, pl.ds(r0, CHUNK), (2 * h + 1) * LANES:(2 * h + 2) * LANES] = o2[CHUNK:2 * CHUNK]
        return carry

    lax.fori_loop(0, n_chunks, out_body, 0, unroll=min(ATTN_UNROLL, n_chunks))


def _attention(z, k_cache, v_cache, cos, sin, q_g, k_g, sinks, q_col, k_col, aw, kvw, tq, mask_prefix):
    bsz, t, _ = z.shape
    assert q_col % aw == 0 and k_col % kvw == 0 and aw // kvw == 4 and kvw % LANES == 0
    k_blk = k_col // kvw
    n_blocks = 2 * (tq // CHUNK) * (kvw // HEAD_DIM)
    assert kvw % HEAD_BLOCK == 0 and aw % HEAD_BLOCK == 0
    idx = jnp.arange(HEAD_BLOCK)
    mean_w = jnp.where(idx[:, None] // HEAD_DIM == idx[None, :] // HEAD_DIM, 1.0 / HEAD_DIM, 0.0).astype(BF16)
    swap_w = (idx[:, None] == (idx[None, :] ^ (HEAD_DIM // 2))).astype(BF16)
    return pl.pallas_call(
        functools.partial(_attn_kernel, mask_prefix=mask_prefix),
        grid=(bsz, t // tq),
        in_specs=[pl.BlockSpec(memory_space=pltpu.SMEM),
                  pl.BlockSpec((1, tq, aw), lambda i, s: (i, s, q_col // aw)),
                  pl.BlockSpec((1, tq, kvw), lambda i, s: (i, s, k_blk)),
                  pl.BlockSpec((1, tq, kvw), lambda i, s: (i, s, k_blk + 1)),
                  pl.BlockSpec((1, WINDOW, kvw), lambda i, s: (i, 0, 0)),
                  pl.BlockSpec((1, WINDOW, kvw), lambda i, s: (i, 0, 0)),
                  pl.BlockSpec((tq, LANES), lambda i, s: (s, 0)),
                  pl.BlockSpec((tq, LANES), lambda i, s: (s, 0)),
                  pl.BlockSpec((1, HEAD_BLOCK), lambda i, s: (0, 0)),
                  pl.BlockSpec((1, HEAD_BLOCK), lambda i, s: (0, 0)),
                  pl.BlockSpec((HEAD_BLOCK, HEAD_BLOCK), lambda i, s: (0, 0)),
                  pl.BlockSpec((HEAD_BLOCK, HEAD_BLOCK), lambda i, s: (0, 0))],
        out_specs=[pl.BlockSpec((1, tq, aw), lambda i, s: (i, s, 0)),
                   pl.BlockSpec((1, WINDOW, kvw), lambda i, s: (i, 0, 0)),
                   pl.BlockSpec((1, WINDOW, kvw), lambda i, s: (i, 0, 0))],
        scratch_shapes=[pltpu.VMEM((WINDOW + tq, kvw), F32),
                        pltpu.VMEM((WINDOW + tq, kvw), F32),
                        pltpu.VMEM((WINDOW + tq + CHUNK, 4 * kvw), BF16),
                        pltpu.VMEM((WINDOW + tq, 4 * kvw), BF16),
                        pltpu.VMEM((tq, aw), BF16),
                        pltpu.VMEM((n_blocks, 2 * CHUNK, WINDOW + 2 * CHUNK), F32),
                        pltpu.VMEM((n_blocks, 2 * CHUNK, WINDOW + 2 * CHUNK), BF16)],
        out_shape=[jax.ShapeDtypeStruct((bsz, t, aw), BF16),
                   jax.ShapeDtypeStruct((bsz, WINDOW, kvw), F32),
                   jax.ShapeDtypeStruct((bsz, WINDOW, kvw), F32)],
        compiler_params=_params(2),
        name="attention",
    )(sinks, z, z, z, k_cache, v_cache, cos, sin, q_g, k_g, mean_w, swap_w)


def _merge_kernel(a_ref, o_ref, gc_ref, ga_ref, wc_ref, wa_ref, wo_ref, x_ref, g1_ref, y_ref, mg):
    nb, tt, d = x_ref.shape
    tn = wc_ref.shape[1]
    n = pl.program_id(2)
    a = a_ref[...].reshape(nb * tt, -1)
    o = o_ref[...].reshape(nb * tt, -1)
    yc = jnp.dot(a, wc_ref[...], preferred_element_type=F32)
    ya = jnp.dot(o, wa_ref[...], preferred_element_type=F32)
    gc = gc_ref[...].astype(F32).reshape(nb * tt, tn)
    ga = ga_ref[...].astype(F32).reshape(nb * tt, tn)
    merged = _sigmoid(gc) * yc + _sigmoid(ga) * ya
    mg[:, pl.ds(pl.multiple_of(n * tn, tn), tn)] = merged.astype(BF16)

    @pl.when(n == pl.num_programs(2) - 1)
    def _():
        proj = jnp.dot(mg[...], wo_ref[...], preferred_element_type=F32).reshape(nb, tt, d)
        y_ref[...] = x_ref[...] + g1_ref[...] * proj


def _merge(a, o, z, gc_col, ga_col, wc, wa, wo, x, mod3, row0, nb, tt, tn):
    bsz, t, d = x.shape
    c = a.shape[2]
    aw = o.shape[2]
    assert gc_col % tn == 0 and ga_col % tn == 0 and d % tn == 0
    return pl.pallas_call(
        _merge_kernel,
        grid=(bsz // nb, t // tt, d // tn),
        in_specs=[pl.BlockSpec((nb, tt, c), lambda i, s, n: (i, s, 0)),
                  pl.BlockSpec((nb, tt, aw), lambda i, s, n: (i, s, 0)),
                  pl.BlockSpec((nb, tt, tn), lambda i, s, n: (i, s, gc_col // tn + n)),
                  pl.BlockSpec((nb, tt, tn), lambda i, s, n: (i, s, ga_col // tn + n)),
                  pl.BlockSpec((c, tn), lambda i, s, n: (0, n)),
                  pl.BlockSpec((aw, tn), lambda i, s, n: (0, n)),
                  pl.BlockSpec((d, d), lambda i, s, n: (0, 0), pipeline_mode=pl.Buffered(1)),
                  pl.BlockSpec((nb, tt, d), lambda i, s, n: (i, s, 0)),
                  pl.BlockSpec((nb, 1, d), lambda i, s, n: (row0 // nb + i, 0, 2))],
        out_specs=pl.BlockSpec((nb, tt, d), lambda i, s, n: (i, s, 0)),
        out_shape=jax.ShapeDtypeStruct((bsz, t, d), F32),
        scratch_shapes=[pltpu.VMEM((nb * tt, d), BF16)],
        compiler_params=_params(3),
        name="merge_out",
    )(a, o, z, z, wc, wa, wo, x, mod3)


FFN_PAD = SUBLANES
FFN_PARTS = 2


def _ffn_kernel(x_ref, sc_ref, sh_ref, g2_ref, ng_ref, wg_ref, wv_ref, cw_ref, cb_ref, wd_ref, ctx_ref,
                y_ref, nf_ref, h_ref, ubuf, halo):
    nb, tt, d = x_ref.shape
    tf = wg_ref.shape[1]
    kw = cw_ref.shape[0]
    t = pl.program_id(1)
    f = pl.program_id(2)

    @pl.when(f == 0)
    def _():
        h = _adaln(x_ref[...], ng_ref[...], sc_ref[...], sh_ref[...])
        h_ref[...] = h.reshape(nb * tt, d).astype(BF16)
        y_ref[...] = jnp.zeros(y_ref.shape, F32)

    @pl.when(t == 0)
    def _():
        ubuf[:, 0:FFN_PAD, :] = ctx_ref[...]

    @pl.when(t > 0)
    def _():
        ubuf[:, 0:FFN_PAD, :] = halo[f]

    cbias = cb_ref[...].reshape(1, 1, tf)
    taps = [cw_ref[k:k + 1, :].reshape(1, 1, tf) for k in range(kw)]
    for bs, ts in _row_parts(nb, tt, FFN_PARTS):
        nbp, ttp = bs.stop - bs.start, ts.stop - ts.start
        r0 = bs.start * tt + ts.start
        hp = h_ref[r0:r0 + nbp * ttp, :]
        ug = jnp.dot(hp, wg_ref[...], preferred_element_type=F32)
        uv = jnp.dot(hp, wv_ref[...], preferred_element_type=F32)
        ubuf[bs, FFN_PAD + ts.start:FFN_PAD + ts.stop, :] = ug.reshape(nbp, ttp, tf)
        conv = jnp.broadcast_to(cbias, (nbp, ttp, tf))
        for k in range(kw):
            lead = FFN_PAD - (kw - 1) + k + ts.start
            conv = conv + ubuf[bs, lead:lead + ttp, :] * taps[k]
        act = (conv * _sigmoid(conv)).reshape(nbp * ttp, tf) * uv
        part = jnp.dot(act.astype(BF16), wd_ref[...], preferred_element_type=F32)
        y_ref[bs, ts, :] += part.reshape(nbp, ttp, d)

    tail = ubuf[:, tt:tt + FFN_PAD, :]
    halo[f] = tail
    nf_ref[:, :, pl.ds(pl.multiple_of(f * tf, tf), tf)] = tail

    @pl.when(f == pl.num_programs(2) - 1)
    def _():
        y_ref[...] = x_ref[...] + g2_ref[...] * y_ref[...]


def _ffn(x, mod3, row0, norm_g, w_up, conv_w, conv_b, w_down, ctx_pad, nb, tt, tf):
    bsz, t, d = x.shape
    dff = w_down.shape[0]
    kw = conv_w.shape[0]
    nf = dff // tf
    assert dff % tf == 0
    return pl.pallas_call(
        _ffn_kernel,
        grid=(bsz // nb, t // tt, nf),
        in_specs=[pl.BlockSpec((nb, tt, d), lambda i, s, f: (i, s, 0), pipeline_mode=pl.Buffered(1)),
                  pl.BlockSpec((nb, 1, d), lambda i, s, f: (row0 // nb + i, 0, 4)),
                  pl.BlockSpec((nb, 1, d), lambda i, s, f: (row0 // nb + i, 0, 3)),
                  pl.BlockSpec((nb, 1, d), lambda i, s, f: (row0 // nb + i, 0, 5)),
                  pl.BlockSpec((1, d), lambda i, s, f: (0, 0)),
                  pl.BlockSpec((d, tf), lambda i, s, f: (0, f)),
                  pl.BlockSpec((d, tf), lambda i, s, f: (0, nf + f)),
                  pl.BlockSpec((kw, tf), lambda i, s, f: (0, f)),
                  pl.BlockSpec((1, tf), lambda i, s, f: (0, f)),
                  pl.BlockSpec((tf, d), lambda i, s, f: (f, 0)),
                  pl.BlockSpec((nb, FFN_PAD, tf), lambda i, s, f: (i, 0, f))],
        out_specs=[pl.BlockSpec((nb, tt, d), lambda i, s, f: (i, s, 0)),
                   pl.BlockSpec((nb, FFN_PAD, dff), lambda i, s, f: (i, 0, 0))],
        out_shape=[jax.ShapeDtypeStruct((bsz, t, d), F32),
                   jax.ShapeDtypeStruct((bsz, FFN_PAD, dff), F32)],
        scratch_shapes=[pltpu.VMEM((nb * tt, d), BF16),
                        pltpu.VMEM((nb, FFN_PAD + tt, tf), F32),
                        pltpu.VMEM((nf, nb, FFN_PAD, tf), F32)],
        compiler_params=_params(3),
        name="conv_ffn",
    )(x, mod3, mod3, mod3, norm_g, w_up, w_up, conv_w, conv_b, w_down, ctx_pad)


def _rope_tables(pos):
    half = HEAD_DIM // 2
    inv_freq = 1.0 / (ROPE_THETA ** (jnp.arange(half, dtype=F32) / half))
    ang = pos.astype(F32)[:, None] * inv_freq[None, :]
    cos = jnp.cos(ang)
    sin = jnp.sin(ang)
    reps = LANES // HEAD_DIM
    return (jnp.tile(jnp.concatenate([cos, cos], axis=-1), (1, reps)),
            jnp.tile(jnp.concatenate([-sin, sin], axis=-1), (1, reps)))


def _front_pad(ctx, rows):
    return jnp.pad(ctx, ((0, 0), (rows - ctx.shape[1], 0), (0, 0)))


def _layer(x, mod3, row0, pos, conv_ctx, k_cache, v_cache, ffn_ctx, mask_prefix, p, tiles):
    bsz, t, d = x.shape
    c = p["conv_w"].shape[1]
    kvw = k_cache.shape[2] * k_cache.shape[3]
    aw = p["attn_o_w"].shape[0]
    nb, tt, tq = tiles["nb"], tiles["tt"], tiles["tq"]
    assert row0 % nb == 0 and bsz % nb == 0 and t % tt == 0 and t % tq == 0
    assert row0 % tiles["nb_merge"] == 0 and bsz % tiles["nb_merge"] == 0

    z = _in_proj(x, mod3, row0, p["norm1_g"], p["w_in"], p["b_in"], nb, tiles["tt_big"], tiles["tn_in"])
    a, nc = _conv_branch(z, _front_pad(conv_ctx, CONV_PAD), p["conv_w"], p["conv_b"], p["ln_g"], p["ln_b"],
                         nb, tt)
    cos, sin = _rope_tables(pos)
    reps = HEAD_BLOCK // HEAD_DIM
    o, nk, nv = _attention(z, k_cache.reshape(bsz, -1, kvw), v_cache.reshape(bsz, -1, kvw), cos, sin,
                           jnp.tile(p["q_norm_g"], reps)[None, :], jnp.tile(p["k_norm_g"], reps)[None, :],
                           p["sinks"], 2 * c, 2 * c + aw + 2 * d, aw, kvw, tq, mask_prefix)
    gc_col = 2 * c + aw
    x1 = _merge(a, o, z, gc_col, gc_col + d, p["conv_out_w"], p["attn_o_w"], p["w_out"], x, mod3, row0,
                tiles["nb_merge"], tt, tiles["tn_merge"])
    y, nf = _ffn(x1, mod3, row0, p["norm2_g"], p["ffn_up_w"], p["ffn_conv_w"], p["ffn_conv_b"],
                 p["ffn_down_w"], _front_pad(ffn_ctx, FFN_PAD), nb, tiles["tt_big"], tiles["tf"])
    kw = p["conv_w"].shape[0]
    fkw = p["ffn_conv_w"].shape[0]
    return (y, nc[:, CONV_PAD - (kw - 1):], nk.reshape(k_cache.shape), nv.reshape(v_cache.shape),
            nf[:, FFN_PAD - (fkw - 1):])


def kernel(x_prompt, x_sample, c_prompt, c_sample, cache_conv, cache_k, cache_v, cache_ffn_conv, mod_w, mod_b, norm1_g, w_in, b_in, conv_w, conv_b, ln_g, ln_b, conv_out_w, q_norm_g, k_norm_g, sinks, attn_o_w, w_out, norm2_g, ffn_up_w, ffn_conv_w, ffn_conv_b, ffn_down_w):
    depth = mod_w.shape[0]
    bp, tp, d = x_prompt.shape
    bs, ts, _ = x_sample.shape
    pos_p = jnp.arange(tp)
    pos_s = PAST_LEN + jnp.arange(ts)
    yp, ys = x_prompt, x_sample
    outs = [[] for _ in range(8)]
    tiles_p = dict(nb=1, nb_merge=1, tt=min(512, tp), tt_big=min(1024, tp), tq=min(512, tp), tn_in=1536, tn_merge=1024,
                   tf=512)
    tiles_s = dict(nb=min(16, bs), nb_merge=min(8, bs), tt=ts, tt_big=ts, tq=ts, tn_in=1536, tn_merge=1024,
                   tf=512)
    for l in range(depth):
        kv0 = 2 * conv_w.shape[2] + attn_o_w.shape[1]
        kv1 = kv0 + 2 * cache_k.shape[3] * cache_k.shape[4]
        reorder = lambda m: jnp.concatenate([m[..., :kv0], m[..., kv1:], m[..., kv0:kv1]], axis=-1)
        p = dict(norm1_g=norm1_g[l][None, :], w_in=reorder(w_in[l]).astype(BF16), b_in=reorder(b_in[l])[None, :],
                 conv_w=conv_w[l], conv_b=conv_b[l][None, :], ln_g=ln_g[l][None, :], ln_b=ln_b[l][None, :],
                 conv_out_w=conv_out_w[l].astype(BF16), q_norm_g=q_norm_g[l], k_norm_g=k_norm_g[l],
                 sinks=sinks[l], attn_o_w=attn_o_w[l].astype(BF16), w_out=w_out[l].astype(BF16),
                 norm2_g=norm2_g[l][None, :], ffn_up_w=ffn_up_w[l].astype(BF16), ffn_conv_w=ffn_conv_w[l],
                 ffn_conv_b=ffn_conv_b[l][None, :], ffn_down_w=ffn_down_w[l].astype(BF16))
        c_all = jnp.concatenate([c_prompt, c_sample], axis=0)
        mod3 = _mod(c_all, mod_w[l], mod_b[l][None, :])[:, None, :]
        zeros_conv = jnp.zeros((bp,) + cache_conv.shape[2:], F32)
        zeros_kv = jnp.zeros((bp,) + cache_k.shape[2:], F32)
        zeros_ffn = jnp.zeros((bp,) + cache_ffn_conv.shape[2:], F32)
        yp, nc_p, nk_p, nv_p, nf_p = _layer(yp, mod3, 0, pos_p, zeros_conv, zeros_kv, zeros_kv, zeros_ffn,
                                            True, p, tiles_p)
        ys, nc_s, nk_s, nv_s, nf_s = _layer(ys, mod3, bp, pos_s, cache_conv[l], cache_k[l], cache_v[l],
                                            cache_ffn_conv[l], False, p, tiles_s)
        for lst, val in zip(outs, (nc_p, nc_s, nk_p, nk_s, nv_p, nv_s, nf_p, nf_s)):
            lst.append(val)
    return (yp, ys) + tuple(jnp.stack(o) for o in outs)
```

```python
import functools

import jax
import jax.numpy as jnp
from jax import lax
from jax.experimental import pallas as pl
from jax.experimental.pallas import tpu as pltpu

CHUNK = 64
HEAD_DIM = 64
WINDOW = 128
PAST_LEN = 1024
ROPE_THETA = 10000.0
EPS = 1e-6
NEG_INF = -1e30
LANES = 128
SUBLANES = 8
VMEM_LIMIT_BYTES = 60 * 1024 * 1024

F32 = jnp.float32
BF16 = jnp.bfloat16


def _params(n_axes):
    return pltpu.CompilerParams(dimension_semantics=("arbitrary",) * n_axes,
                                vmem_limit_bytes=VMEM_LIMIT_BYTES)


def _sigmoid(x):
    return 1.0 / (1.0 + jnp.exp(-x))


def _mod_kernel(c_ref, w_ref, b_ref, o_ref):
    c = c_ref[...]
    a = (c * _sigmoid(c)).astype(BF16)
    o_ref[...] = jnp.dot(a, w_ref[...].astype(BF16), preferred_element_type=F32) + b_ref[...]


def _mod(c_all, mod_w, mod_b, tn=1024):
    m, d = c_all.shape
    n = mod_w.shape[1]
    return pl.pallas_call(
        _mod_kernel,
        grid=(n // tn,),
        in_specs=[pl.BlockSpec((m, d), lambda j: (0, 0)),
                  pl.BlockSpec((d, tn), lambda j: (0, j)),
                  pl.BlockSpec((1, tn), lambda j: (0, j))],
        out_specs=pl.BlockSpec((m, tn), lambda j: (0, j)),
        out_shape=jax.ShapeDtypeStruct((m, n), F32),
        compiler_params=_params(1),
        name="mod",
    )(c_all, mod_w, mod_b)


PROLOGUE_PARTS = 4


def _row_parts(nb, tt, parts):
    if nb >= parts:
        step = nb // parts
        return [(slice(i * step, (i + 1) * step), slice(0, tt)) for i in range(parts)]
    step = tt // parts
    return [(slice(b, b + 1), slice(i * step, (i + 1) * step)) for b in range(nb) for i in range(parts)]


def _adaln(x, g, sc, sh):
    ms = jnp.mean(x * x, axis=-1, keepdims=True)
    y = x * lax.rsqrt(ms + EPS) * g
    return y * (1.0 + sc) + sh


def _in_kernel(x_ref, sc_ref, sh_ref, g_ref, w_ref, b_ref, o_ref, h_ref):
    nb, tt, d = x_ref.shape
    j = pl.program_id(2)

    @pl.when(j == 0)
    def _():
        for bs, ts in _row_parts(nb, tt, PROLOGUE_PARTS):
            rows = (bs.stop - bs.start) * (ts.stop - ts.start)
            r0 = bs.start * tt + ts.start
            h = _adaln(x_ref[bs, ts, :], g_ref[...], sc_ref[bs], sh_ref[bs]).reshape(rows, d).astype(BF16)
            h_ref[r0:r0 + rows, :] = h
            z = jnp.dot(h, w_ref[...], preferred_element_type=F32) + b_ref[...]
            o_ref[bs, ts, :] = z.reshape(bs.stop - bs.start, ts.stop - ts.start, -1).astype(o_ref.dtype)

    @pl.when(j > 0)
    def _():
        z = jnp.dot(h_ref[...], w_ref[...], preferred_element_type=F32) + b_ref[...]
        o_ref[...] = z.reshape(nb, tt, -1).astype(o_ref.dtype)


def _in_proj(x, mod3, row0, norm_g, w, b, nb, tt, tn):
    bsz, t, d = x.shape
    n = w.shape[1]
    return pl.pallas_call(
        _in_kernel,
        grid=(bsz // nb, t // tt, n // tn),
        in_specs=[pl.BlockSpec((nb, tt, d), lambda i, s, j: (i, s, 0)),
                  pl.BlockSpec((nb, 1, d), lambda i, s, j: (row0 // nb + i, 0, 1)),
                  pl.BlockSpec((nb, 1, d), lambda i, s, j: (row0 // nb + i, 0, 0)),
                  pl.BlockSpec((1, d), lambda i, s, j: (0, 0)),
                  pl.BlockSpec((d, tn), lambda i, s, j: (0, j)),
                  pl.BlockSpec((1, tn), lambda i, s, j: (0, j))],
        out_specs=pl.BlockSpec((nb, tt, tn), lambda i, s, j: (i, s, j)),
        out_shape=jax.ShapeDtypeStruct((bsz, t, n), BF16),
        scratch_shapes=[pltpu.VMEM((nb * tt, d), BF16)],
        compiler_params=_params(3),
        name="in_proj",
    )(x, mod3, mod3, norm_g, w, b)


CONV_PAD = 32
CONV_STEPS = 16


def _slab_rows(rows):
    pitch = -(-rows // SUBLANES) * SUBLANES
    return pitch if (pitch // SUBLANES) % 2 else pitch + SUBLANES


def _conv_kernel(za_ref, zb_ref, ctx_ref, w_ref, b_ref, lg_ref, lb_ref, a_ref, nc_ref, gbuf, dwbuf):
    nb, tt, c = za_ref.shape
    nct = c // LANES
    kw = w_ref.shape[0]
    tp = gbuf.shape[1] // nct
    tp2 = dwbuf.shape[1] // nct
    lead = CONV_PAD - (kw - 1)
    t = pl.program_id(1)

    @pl.when(t == 0)
    def _():
        for j in range(nct):
            gbuf[:, j * tp:j * tp + CONV_PAD, :] = ctx_ref[:, :, j * LANES:(j + 1) * LANES]

    @pl.when(t > 0)
    def _():
        for j in range(nct):
            gbuf[:, j * tp:j * tp + CONV_PAD, :] = gbuf[:, j * tp + tt:j * tp + tt + CONV_PAD, :]

    for j in range(nct):
        ls = slice(j * LANES, (j + 1) * LANES)
        glu = za_ref[:, :, ls].astype(F32) * _sigmoid(zb_ref[:, :, ls].astype(F32))
        gbuf[:, j * tp + CONV_PAD:j * tp + CONV_PAD + tt, :] = glu
        nc_ref[:, :, ls] = gbuf[:, j * tp + tt:j * tp + tt + CONV_PAD, :]

    w = [w_ref[k] for k in range(kw)]
    bias = b_ref[...]
    nblk = tt // CONV_STEPS

    def body(i, carry):
        n = i // nblk
        t0 = (i % nblk) * CONV_STEPS
        acc = [bias] * CONV_STEPS
        for m in range(CONV_STEPS + kw - 1):
            g = gbuf[n, pl.ds(t0 + lead + m, SUBLANES, stride=tp), :]
            for s in range(max(0, m - (kw - 1)), min(CONV_STEPS - 1, m) + 1):
                acc[s] = acc[s] + g * w[m - s]
        for s in range(CONV_STEPS):
            dwbuf[n, pl.ds(t0 + s, SUBLANES, stride=tp2), :] = acc[s]
        return carry

    lax.fori_loop(0, nb * nblk, body, 0)

    slabs = [dwbuf[:, j * tp2:j * tp2 + tt, :] for j in range(nct)]
    mu = jnp.sum(functools.reduce(jnp.add, slabs), axis=-1, keepdims=True) * (1.0 / c)
    cen = [d - mu for d in slabs]
    var = jnp.sum(functools.reduce(jnp.add, [x * x for x in cen]), axis=-1, keepdims=True) * (1.0 / c)
    inv = lax.rsqrt(var + EPS)
    for j in range(nct):
        ls = slice(j * LANES, (j + 1) * LANES)
        y = cen[j] * inv * lg_ref[:, ls] + lb_ref[:, ls]
        a_ref[:, :, ls] = (y * _sigmoid(y)).astype(a_ref.dtype)


def _conv_branch(z, ctx_pad, conv_w, conv_b, ln_g, ln_b, nb, tt):
    bsz, t, _ = z.shape
    kw, c = conv_w.shape
    nct = c // LANES
    assert nct == SUBLANES and tt % CONV_STEPS == 0
    tp = _slab_rows(CONV_PAD + tt)
    tp2 = _slab_rows(tt)
    return pl.pallas_call(
        _conv_kernel,
        grid=(bsz // nb, t // tt),
        in_specs=[pl.BlockSpec((nb, tt, c), lambda i, s: (i, s, 0)),
                  pl.BlockSpec((nb, tt, c), lambda i, s: (i, s, 1)),
                  pl.BlockSpec((nb, CONV_PAD, c), lambda i, s: (i, 0, 0)),
                  pl.BlockSpec((kw, nct, LANES), lambda i, s: (0, 0, 0)),
                  pl.BlockSpec((nct, LANES), lambda i, s: (0, 0)),
                  pl.BlockSpec((1, c), lambda i, s: (0, 0)),
                  pl.BlockSpec((1, c), lambda i, s: (0, 0))],
        out_specs=[pl.BlockSpec((nb, tt, c), lambda i, s: (i, s, 0)),
                   pl.BlockSpec((nb, CONV_PAD, c), lambda i, s: (i, 0, 0))],
        out_shape=[jax.ShapeDtypeStruct((bsz, t, c), BF16),
                   jax.ShapeDtypeStruct((bsz, CONV_PAD, c), F32)],
        scratch_shapes=[pltpu.VMEM((nb, nct * tp, LANES), F32),
                        pltpu.VMEM((nb, nct * tp2, LANES), F32)],
        compiler_params=_params(2),
        name="conv_branch",
    )(z, z, ctx_pad, conv_w.reshape(kw, nct, LANES), conv_b.reshape(nct, LANES), ln_g, ln_b)


HEAD_BLOCK = 256
ATTN_UNROLL = 4


def _split_dot(x, w, split):
    hi = x.astype(BF16)
    out = jnp.dot(hi, w, preferred_element_type=F32)
    if split:
        lo = (x - hi.astype(F32)).astype(BF16)
        out = out + jnp.dot(lo, w, preferred_element_type=F32)
    return out


def _heads_norm_rope(x, g, cos, sin, mean_w, swap_w, split):
    ms = _split_dot(x * x, mean_w, split)
    y = x * lax.rsqrt(ms + EPS) * g
    return y * cos + _split_dot(y, swap_w, split) * sin


def _attn_kernel(sink_ref, q_ref, k_ref, v_ref, kc_ref, vc_ref, cos_ref, sin_ref, qg_ref, kg_ref, mw_ref, sw_ref,
                 o_ref, nk_ref, nv_ref, kf, vf, kx, vx, qs, s_scr, p_scr, *, mask_prefix):
    tq = q_ref.shape[1]
    kvw = k_ref.shape[2]
    n_kv = kvw // HEAD_DIM
    t = pl.program_id(1)
    reps = HEAD_BLOCK // LANES
    cos = jnp.concatenate([cos_ref[...]] * reps, axis=1)
    sin = jnp.concatenate([sin_ref[...]] * reps, axis=1)

    @pl.when(t == 0)
    def _():
        kf[0:WINDOW, :] = kc_ref[0]
        vf[0:WINDOW, :] = vc_ref[0]

    @pl.when(t > 0)
    def _():
        kf[0:WINDOW, :] = kf[tq:tq + WINDOW, :]
        vf[0:WINDOW, :] = vf[tq:tq + WINDOW, :]

    for j in range(kvw // HEAD_BLOCK):
        ls = slice(j * HEAD_BLOCK, (j + 1) * HEAD_BLOCK)
        kf[WINDOW:WINDOW + tq, ls] = _heads_norm_rope(k_ref[0, :, ls].astype(F32), kg_ref[...], cos, sin,
                                                     mw_ref[...], sw_ref[...], True)
    vf[WINDOW:WINDOW + tq, :] = v_ref[0].astype(F32)
    nk_ref[0] = kf[tq:tq + WINDOW, :]
    nv_ref[0] = vf[tq:tq + WINDOW, :]

    n_buf = WINDOW + tq
    lane = lax.broadcasted_iota(jnp.int32, (n_buf, LANES), 1)
    first = lane < HEAD_DIM
    for src, dst, fill in ((kf, kx, 0.0), (vf, vx, 1.0)):
        for j in range(kvw // LANES):
            tile = src[:, j * LANES:(j + 1) * LANES]
            swapped = pltpu.roll(tile, HEAD_DIM, 1)
            variants = (jnp.where(first, tile, fill), jnp.where(first, fill, swapped),
                        jnp.where(first, swapped, fill), jnp.where(first, fill, tile))
            for m, val in enumerate(variants):
                dst[0:n_buf, (4 * j + m) * LANES:(4 * j + m + 1) * LANES] = val.astype(BF16)
    kx[n_buf:n_buf + CHUNK, :] = jnp.zeros((CHUNK, kx.shape[1]), BF16)

    log2e = 1.4426950408889634
    scale = HEAD_DIM ** -0.5 * log2e
    for j in range(q_ref.shape[2] // HEAD_BLOCK):
        ls = slice(j * HEAD_BLOCK, (j + 1) * HEAD_BLOCK)
        qn = _heads_norm_rope(q_ref[0, :, ls].astype(F32), qg_ref[...], cos, sin,
                              mw_ref[...], sw_ref[...], False)
        qs[:, ls] = (qn * scale).astype(BF16)

    n_keys = WINDOW + CHUNK
    n_cols = n_keys + CHUNK
    n_chunks = tq // CHUNK
    rows = 2 * CHUNK
    dn = (((1,), (1,)), ((), ()))
    lane2 = lax.broadcasted_iota(jnp.int32, (rows, LANES), 1)
    row2 = lax.broadcasted_iota(jnp.int32, (rows, LANES), 0)
    first2 = lane2 < HEAD_DIM

    def score_body(c, carry):
        r0 = pl.multiple_of(c * CHUNK, CHUNK)
        for h in range(n_kv):
            q2 = jnp.concatenate([qs[pl.ds(r0, CHUNK), (2 * h) * LANES:(2 * h + 1) * LANES],
                                  qs[pl.ds(r0, CHUNK), (2 * h + 1) * LANES:(2 * h + 2) * LANES]], axis=0)
            for v in range(2):
                kt = kx[pl.ds(r0, n_cols), (2 * h + v) * LANES:(2 * h + v + 1) * LANES]
                s = lax.dot_general(q2, kt, dn, preferred_element_type=F32)
                sink = jnp.where(row2 < CHUNK, sink_ref[4 * h + v], sink_ref[4 * h + 2 + v]) * log2e
                pad = jnp.where(lane2 == n_keys - LANES, sink, NEG_INF)
                blk = 2 * (c * n_kv + h) + v
                s_scr[blk, :, 0:LANES] = s[:, 0:LANES]
                s_scr[blk, :, LANES:n_cols] = jnp.where(first2, s[:, LANES:n_cols], pad)
        return carry

    lax.fori_loop(0, n_chunks, score_body, 0, unroll=min(ATTN_UNROLL, n_chunks))

    if mask_prefix:
        for c in range(min(2, n_chunks)):
            g = t * n_chunks + c
            blocks = slice(2 * c * n_kv, 2 * (c + 1) * n_kv)

            @pl.when(g == 0)
            def _():
                s_scr[blocks, :, 0:LANES] = jnp.full((2 * n_kv, rows, LANES), NEG_INF, F32)

            @pl.when(g == 1)
            def _():
                s_scr[blocks, :, 0:LANES] = jnp.where(first2[None], NEG_INF, s_scr[blocks, :, 0:LANES])

    s_all = s_scr[...]
    m_all = jnp.max(s_all, axis=-1, keepdims=True)
    p_scr[...] = jnp.exp2(s_all - m_all).astype(BF16)

    e_row = lax.broadcasted_iota(jnp.int32, (CHUNK, LANES), 0) == 0
    e_lane = lax.broadcasted_iota(jnp.int32, (CHUNK, LANES), 1) < HEAD_DIM
    e_lo = jnp.where(jnp.logical_and(e_row, jnp.logical_not(e_lane)), 1.0, 0.0).astype(BF16)
    e_hi = jnp.where(jnp.logical_and(e_row, e_lane), 1.0, 0.0).astype(BF16)

    def out_body(c, carry):
        r0 = pl.multiple_of(c * CHUNK, CHUNK)
        for h in range(n_kv):
            blk = 2 * (c * n_kv + h)
            v_lo = jnp.concatenate([vx[pl.ds(r0, n_keys), (2 * h) * LANES:(2 * h + 1) * LANES], e_lo], axis=0)
            v_hi = jnp.concatenate([vx[pl.ds(r0, n_keys), (2 * h + 1) * LANES:(2 * h + 2) * LANES], e_hi], axis=0)
            o_lo = jnp.dot(p_scr[blk], v_lo, preferred_element_type=F32)
            o_hi = jnp.dot(p_scr[blk + 1], v_hi, preferred_element_type=F32)
            num = jnp.where(first2, o_lo, o_hi)
            den = pltpu.roll(jnp.where(first2, o_hi, o_lo), HEAD_DIM, 1)
            o2 = (num / den).astype(o_ref.dtype)
            o_ref[0, pl.ds(r0, CHUNK), (2 * h) * LANES:(2 * h + 1) * LANES] = o2[0:CHUNK]
            o_ref[0, pl.ds(r0, CHUNK), (2 * h + 1) * LANES:(2 * h + 2) * LANES] = o2[CHUNK:2 * CHUNK]
        return carry

    lax.fori_loop(0, n_chunks, out_body, 0, unroll=min(ATTN_UNROLL, n_chunks))


def _attention(z, k_cache, v_cache, cos, sin, q_g, k_g, sinks, q_col, k_col, aw, kvw, tq, mask_prefix):
    bsz, t, _ = z.shape
    assert q_col % aw == 0 and k_col % kvw == 0 and aw // kvw == 4 and kvw % LANES == 0
    k_blk = k_col // kvw
    n_blocks = 2 * (tq // CHUNK) * (kvw // HEAD_DIM)
    assert kvw % HEAD_BLOCK == 0 and aw % HEAD_BLOCK == 0
    idx = jnp.arange(HEAD_BLOCK)
    mean_w = jnp.where(idx[:, None] // HEAD_DIM == idx[None, :] // HEAD_DIM, 1.0 / HEAD_DIM, 0.0).astype(BF16)
    swap_w = (idx[:, None] == (idx[None, :] ^ (HEAD_DIM // 2))).astype(BF16)
    return pl.pallas_call(
        functools.partial(_attn_kernel, mask_prefix=mask_prefix),
        grid=(bsz, t // tq),
        in_specs=[pl.BlockSpec(memory_space=pltpu.SMEM),
                  pl.BlockSpec((1, tq, aw), lambda i, s: (i, s, q_col // aw)),
                  pl.BlockSpec((1, tq, kvw), lambda i, s: (i, s, k_blk)),
                  pl.BlockSpec((1, tq, kvw), lambda i, s: (i, s, k_blk + 1)),
                  pl.BlockSpec((1, WINDOW, kvw), lambda i, s: (i, 0, 0)),
                  pl.BlockSpec((1, WINDOW, kvw), lambda i, s: (i, 0, 0)),
                  pl.BlockSpec((tq, LANES), lambda i, s: (s, 0)),
                  pl.BlockSpec((tq, LANES), lambda i, s: (s, 0)),
                  pl.BlockSpec((1, HEAD_BLOCK), lambda i, s: (0, 0)),
                  pl.BlockSpec((1, HEAD_BLOCK), lambda i, s: (0, 0)),
                  pl.BlockSpec((HEAD_BLOCK, HEAD_BLOCK), lambda i, s: (0, 0)),
                  pl.BlockSpec((HEAD_BLOCK, HEAD_BLOCK), lambda i, s: (0, 0))],
        out_specs=[pl.BlockSpec((1, tq, aw), lambda i, s: (i, s, 0)),
                   pl.BlockSpec((1, WINDOW, kvw), lambda i, s: (i, 0, 0)),
                   pl.BlockSpec((1, WINDOW, kvw), lambda i, s: (i, 0, 0))],
        scratch_shapes=[pltpu.VMEM((WINDOW + tq, kvw), F32),
                        pltpu.VMEM((WINDOW + tq, kvw), F32),
                        pltpu.VMEM((WINDOW + tq + CHUNK, 4 * kvw), BF16),
                        pltpu.VMEM((WINDOW + tq, 4 * kvw), BF16),
                        pltpu.VMEM((tq, aw), BF16),
                        pltpu.VMEM((n_blocks, 2 * CHUNK, WINDOW + 2 * CHUNK), F32),
                        pltpu.VMEM((n_blocks, 2 * CHUNK, WINDOW + 2 * CHUNK), BF16)],
        out_shape=[jax.ShapeDtypeStruct((bsz, t, aw), BF16),
                   jax.ShapeDtypeStruct((bsz, WINDOW, kvw), F32),
                   jax.ShapeDtypeStruct((bsz, WINDOW, kvw), F32)],
        compiler_params=_params(2),
        name="attention",
    )(sinks, z, z, z, k_cache, v_cache, cos, sin, q_g, k_g, mean_w, swap_w)


def _merge_kernel(a_ref, o_ref, gc_ref, ga_ref, wc_ref, wa_ref, wo_ref, x_ref, g1_ref, y_ref, mg):
    nb, tt, d = x_ref.shape
    tn = wc_ref.shape[1]
    n = pl.program_id(2)
    a = a_ref[...].reshape(nb * tt, -1)
    o = o_ref[...].reshape(nb * tt, -1)
    yc = jnp.dot(a, wc_ref[...], preferred_element_type=F32)
    ya = jnp.dot(o, wa_ref[...], preferred_element_type=F32)
    gc = gc_ref[...].astype(F32).reshape(nb * tt, tn)
    ga = ga_ref[...].astype(F32).reshape(nb * tt, tn)
    merged = _sigmoid(gc) * yc + _sigmoid(ga) * ya
    mg[:, pl.ds(pl.multiple_of(n * tn, tn), tn)] = merged.astype(BF16)

    @pl.when(n == pl.num_programs(2) - 1)
    def _():
        proj = jnp.dot(mg[...], wo_ref[...], preferred_element_type=F32).reshape(nb, tt, d)
        y_ref[...] = x_ref[...] + g1_ref[...] * proj


def _merge(a, o, z, gc_col, ga_col, wc, wa, wo, x, mod3, row0, nb, tt, tn):
    bsz, t, d = x.shape
    c = a.shape[2]
    aw = o.shape[2]
    assert gc_col % tn == 0 and ga_col % tn == 0 and d % tn == 0
    return pl.pallas_call(
        _merge_kernel,
        grid=(bsz // nb, t // tt, d // tn),
        in_specs=[pl.BlockSpec((nb, tt, c), lambda i, s, n: (i, s, 0)),
                  pl.BlockSpec((nb, tt, aw), lambda i, s, n: (i, s, 0)),
                  pl.BlockSpec((nb, tt, tn), lambda i, s, n: (i, s, gc_col // tn + n)),
                  pl.BlockSpec((nb, tt, tn), lambda i, s, n: (i, s, ga_col // tn + n)),
                  pl.BlockSpec((c, tn), lambda i, s, n: (0, n)),
                  pl.BlockSpec((aw, tn), lambda i, s, n: (0, n)),
                  pl.BlockSpec((d, d), lambda i, s, n: (0, 0), pipeline_mode=pl.Buffered(1)),
                  pl.BlockSpec((nb, tt, d), lambda i, s, n: (i, s, 0)),
                  pl.BlockSpec((nb, 1, d), lambda i, s, n: (row0 // nb + i, 0, 2))],
        out_specs=pl.BlockSpec((nb, tt, d), lambda i, s, n: (i, s, 0)),
        out_shape=jax.ShapeDtypeStruct((bsz, t, d), F32),
        scratch_shapes=[pltpu.VMEM((nb * tt, d), BF16)],
        compiler_params=_params(3),
        name="merge_out",
    )(a, o, z, z, wc, wa, wo, x, mod3)


FFN_PAD = SUBLANES
FFN_PARTS = 2


def _ffn_kernel(x_ref, sc_ref, sh_ref, g2_ref, ng_ref, wg_ref, wv_ref, cw_ref, cb_ref, wd_ref, ctx_ref,
                y_ref, nf_ref, h_ref, ubuf, halo):
    nb, tt, d = x_ref.shape
    tf = wg_ref.shape[1]
    kw = cw_ref.shape[0]
    t = pl.program_id(1)
    f = pl.program_id(2)

    @pl.when(t == 0)
    def _():
        ubuf[:, 0:FFN_PAD, :] = ctx_ref[...]

    @pl.when(t > 0)
    def _():
        ubuf[:, 0:FFN_PAD, :] = halo[f]

    def step(first):
        cbias = cb_ref[...].reshape(1, 1, tf)
        taps = [cw_ref[k:k + 1, :].reshape(1, 1, tf) for k in range(kw)]
        for bs, ts in _row_parts(nb, tt, FFN_PARTS):
            nbp, ttp = bs.stop - bs.start, ts.stop - ts.start
            r0 = bs.start * tt + ts.start
            if first:
                hp = _adaln(x_ref[bs, ts, :], ng_ref[...], sc_ref[bs], sh_ref[bs])
                hp = hp.reshape(nbp * ttp, d).astype(BF16)
                h_ref[r0:r0 + nbp * ttp, :] = hp
            else:
                hp = h_ref[r0:r0 + nbp * ttp, :]
            ug = jnp.dot(hp, wg_ref[...], preferred_element_type=F32)
            uv = jnp.dot(hp, wv_ref[...], preferred_element_type=F32)
            ubuf[bs, FFN_PAD + ts.start:FFN_PAD + ts.stop, :] = ug.reshape(nbp, ttp, tf)
            conv = jnp.broadcast_to(cbias, (nbp, ttp, tf))
            for k in range(kw):
                lead = FFN_PAD - (kw - 1) + k + ts.start
                conv = conv + ubuf[bs, lead:lead + ttp, :] * taps[k]
            act = (conv * _sigmoid(conv)).reshape(nbp * ttp, tf) * uv
            part = jnp.dot(act.astype(BF16), wd_ref[...], preferred_element_type=F32).reshape(nbp, ttp, d)
            if first:
                y_ref[bs, ts, :] = part
            else:
                y_ref[bs, ts, :] += part
        tail = ubuf[:, tt:tt + FFN_PAD, :]
        halo[f] = tail
        nf_ref[:, :, pl.ds(pl.multiple_of(f * tf, tf), tf)] = tail

    pl.when(f == 0)(functools.partial(step, True))
    pl.when(f > 0)(functools.partial(step, False))

    @pl.when(f == pl.num_programs(2) - 1)
    def _():
        y_ref[...] = x_ref[...] + g2_ref[...] * y_ref[...]


def _ffn(x, mod3, row0, norm_g, w_up, conv_w, conv_b, w_down, ctx_pad, nb, tt, tf):
    bsz, t, d = x.shape
    dff = w_down.shape[0]
    kw = conv_w.shape[0]
    nf = dff // tf
    assert dff % tf == 0
    return pl.pallas_call(
        _ffn_kernel,
        grid=(bsz // nb, t // tt, nf),
        in_specs=[pl.BlockSpec((nb, tt, d), lambda i, s, f: (i, s, 0)),
                  pl.BlockSpec((nb, 1, d), lambda i, s, f: (row0 // nb + i, 0, 4)),
                  pl.BlockSpec((nb, 1, d), lambda i, s, f: (row0 // nb + i, 0, 3)),
                  pl.BlockSpec((nb, 1, d), lambda i, s, f: (row0 // nb + i, 0, 5)),
                  pl.BlockSpec((1, d), lambda i, s, f: (0, 0)),
                  pl.BlockSpec((d, tf), lambda i, s, f: (0, f)),
                  pl.BlockSpec((d, tf), lambda i, s, f: (0, nf + f)),
                  pl.BlockSpec((kw, tf), lambda i, s, f: (0, f)),
                  pl.BlockSpec((1, tf), lambda i, s, f: (0, f)),
                  pl.BlockSpec((tf, d), lambda i, s, f: (f, 0)),
                  pl.BlockSpec((nb, FFN_PAD, tf), lambda i, s, f: (i, 0, f))],
        out_specs=[pl.BlockSpec((nb, tt, d), lambda i, s, f: (i, s, 0)),
                   pl.BlockSpec((nb, FFN_PAD, dff), lambda i, s, f: (i, 0, 0))],
        out_shape=[jax.ShapeDtypeStruct((bsz, t, d), F32),
                   jax.ShapeDtypeStruct((bsz, FFN_PAD, dff), F32)],
        scratch_shapes=[pltpu.VMEM((nb * tt, d), BF16),
                        pltpu.VMEM((nb, FFN_PAD + tt, tf), F32),
                        pltpu.VMEM((nf, nb, FFN_PAD, tf), F32)],
        compiler_params=_params(3),
        name="conv_ffn",
    )(x, mod3, mod3, mod3, norm_g, w_up, w_up, conv_w, conv_b, w_down, ctx_pad)


def _rope_tables(pos):
    half = HEAD_DIM // 2
    inv_freq = 1.0 / (ROPE_THETA ** (jnp.arange(half, dtype=F32) / half))
    ang = pos.astype(F32)[:, None] * inv_freq[None, :]
    cos = jnp.cos(ang)
    sin = jnp.sin(ang)
    reps = LANES // HEAD_DIM
    return (jnp.tile(jnp.concatenate([cos, cos], axis=-1), (1, reps)),
            jnp.tile(jnp.concatenate([-sin, sin], axis=-1), (1, reps)))


def _front_pad(ctx, rows):
    return jnp.pad(ctx, ((0, 0), (rows - ctx.shape[1], 0), (0, 0)))


def _layer(x, mod3, row0, pos, conv_ctx, k_cache, v_cache, ffn_ctx, mask_prefix, p, tiles):
    bsz, t, d = x.shape
    c = p["conv_w"].shape[1]
    kvw = k_cache.shape[2] * k_cache.shape[3]
    aw = p["attn_o_w"].shape[0]
    nb, tt, tq = tiles["nb"], tiles["tt"], tiles["tq"]
    assert row0 % nb == 0 and bsz % nb == 0 and t % tt == 0 and t % tq == 0
    assert row0 % tiles["nb_merge"] == 0 and bsz % tiles["nb_merge"] == 0

    z = _in_proj(x, mod3, row0, p["norm1_g"], p["w_in"], p["b_in"], nb, tiles["tt_big"], tiles["tn_in"])
    a, nc = _conv_branch(z, _front_pad(conv_ctx, CONV_PAD), p["conv_w"], p["conv_b"], p["ln_g"], p["ln_b"],
                         nb, tt)
    cos, sin = _rope_tables(pos)
    reps = HEAD_BLOCK // HEAD_DIM
    o, nk, nv = _attention(z, k_cache.reshape(bsz, -1, kvw), v_cache.reshape(bsz, -1, kvw), cos, sin,
                           jnp.tile(p["q_norm_g"], reps)[None, :], jnp.tile(p["k_norm_g"], reps)[None, :],
                           p["sinks"], 2 * c, 2 * c + aw + 2 * d, aw, kvw, tq, mask_prefix)
    gc_col = 2 * c + aw
    x1 = _merge(a, o, z, gc_col, gc_col + d, p["conv_out_w"], p["attn_o_w"], p["w_out"], x, mod3, row0,
                tiles["nb_merge"], tt, tiles["tn_merge"])
    y, nf = _ffn(x1, mod3, row0, p["norm2_g"], p["ffn_up_w"], p["ffn_conv_w"], p["ffn_conv_b"],
                 p["ffn_down_w"], _front_pad(ffn_ctx, FFN_PAD), nb, tiles["tt_big"], tiles["tf"])
    kw = p["conv_w"].shape[0]
    fkw = p["ffn_conv_w"].shape[0]
    return (y, nc[:, CONV_PAD - (kw - 1):], nk.reshape(k_cache.shape), nv.reshape(v_cache.shape),
            nf[:, FFN_PAD - (fkw - 1):])


def kernel(x_prompt, x_sample, c_prompt, c_sample, cache_conv, cache_k, cache_v, cache_ffn_conv, mod_w, mod_b, norm1_g, w_in, b_in, conv_w, conv_b, ln_g, ln_b, conv_out_w, q_norm_g, k_norm_g, sinks, attn_o_w, w_out, norm2_g, ffn_up_w, ffn_conv_w, ffn_conv_b, ffn_down_w):
    depth = mod_w.shape[0]
    bp, tp, d = x_prompt.shape
    bs, ts, _ = x_sample.shape
    pos_p = jnp.arange(tp)
    pos_s = PAST_LEN + jnp.arange(ts)
    yp, ys = x_prompt, x_sample
    outs = [[] for _ in range(8)]
    tiles_p = dict(nb=1, nb_merge=1, tt=min(512, tp), tt_big=min(1024, tp), tq=min(512, tp), tn_in=1920, tn_merge=1024,
                   tf=512)
    tiles_s = dict(nb=min(16, bs), nb_merge=min(8, bs), tt=ts, tt_big=ts, tq=ts, tn_in=1920, tn_merge=1024,
                   tf=512)
    for l in range(depth):
        kv0 = 2 * conv_w.shape[2] + attn_o_w.shape[1]
        kv1 = kv0 + 2 * cache_k.shape[3] * cache_k.shape[4]
        reorder = lambda m: jnp.concatenate([m[..., :kv0], m[..., kv1:], m[..., kv0:kv1]], axis=-1)
        p = dict(norm1_g=norm1_g[l][None, :], w_in=reorder(w_in[l]).astype(BF16), b_in=reorder(b_in[l])[None, :],
                 conv_w=conv_w[l], conv_b=conv_b[l][None, :], ln_g=ln_g[l][None, :], ln_b=ln_b[l][None, :],
                 conv_out_w=conv_out_w[l].astype(BF16), q_norm_g=q_norm_g[l], k_norm_g=k_norm_g[l],
                 sinks=sinks[l], attn_o_w=attn_o_w[l].astype(BF16), w_out=w_out[l].astype(BF16),
                 norm2_g=norm2_g[l][None, :], ffn_up_w=ffn_up_w[l].astype(BF16), ffn_conv_w=ffn_conv_w[l],
                 ffn_conv_b=ffn_conv_b[l][None, :], ffn_down_w=ffn_down_w[l].astype(BF16))
        c_all = jnp.concatenate([c_prompt, c_sample], axis=0)
        mod3 = _mod(c_all, mod_w[l], mod_b[l][None, :])[:, None, :]
        zeros_conv = jnp.zeros((bp,) + cache_conv.shape[2:], F32)
        zeros_kv = jnp.zeros((bp,) + cache_k.shape[2:], F32)
        zeros_ffn = jnp.zeros((bp,) + cache_ffn_conv.shape[2:], F32)
        yp, nc_p, nk_p, nv_p, nf_p = _layer(yp, mod3, 0, pos_p, zeros_conv, zeros_kv, zeros_kv, zeros_ffn,
                                            True, p, tiles_p)
        ys, nc_s, nk_s, nv_s, nf_s = _layer(ys, mod3, bp, pos_s, cache_conv[l], cache_k[l], cache_v[l],
                                            cache_ffn_conv[l], False, p, tiles_s)
        for lst, val in zip(outs, (nc_p, nc_s, nk_p, nk_s, nv_p, nv_s, nf_p, nf_s)):
            lst.append(val)
    return (yp, ys) + tuple(jnp.stack(o) for o in outs)
```

```python
import functools

import jax
import jax.numpy as jnp
from jax import lax
from jax.experimental import pallas as pl
from jax.experimental.pallas import tpu as pltpu

CHUNK = 64
HEAD_DIM = 64
WINDOW = 128
PAST_LEN = 1024
ROPE_THETA = 10000.0
EPS = 1e-6
NEG_INF = -1e30
LANES = 128
SUBLANES = 8
VMEM_LIMIT_BYTES = 60 * 1024 * 1024

F32 = jnp.float32
BF16 = jnp.bfloat16


def _params(n_axes):
    return pltpu.CompilerParams(dimension_semantics=("arbitrary",) * n_axes,
                                vmem_limit_bytes=VMEM_LIMIT_BYTES)


def _sigmoid(x):
    return 1.0 / (1.0 + jnp.exp(-x))


def _mod_kernel(c_ref, w_ref, b_ref, o_ref):
    c = c_ref[...]
    a = (c * _sigmoid(c)).astype(BF16)
    o_ref[...] = jnp.dot(a, w_ref[...].astype(BF16), preferred_element_type=F32) + b_ref[...]


def _mod(c_all, mod_w, mod_b, tn=1024):
    m, d = c_all.shape
    n = mod_w.shape[1]
    return pl.pallas_call(
        _mod_kernel,
        grid=(n // tn,),
        in_specs=[pl.BlockSpec((m, d), lambda j: (0, 0)),
                  pl.BlockSpec((d, tn), lambda j: (0, j)),
                  pl.BlockSpec((1, tn), lambda j: (0, j))],
        out_specs=pl.BlockSpec((m, tn), lambda j: (0, j)),
        out_shape=jax.ShapeDtypeStruct((m, n), F32),
        compiler_params=_params(1),
        name="mod",
    )(c_all, mod_w, mod_b)


PROLOGUE_PARTS = 4


def _row_parts(nb, tt, parts):
    if nb >= parts:
        step = nb // parts
        return [(slice(i * step, (i + 1) * step), slice(0, tt)) for i in range(parts)]
    step = tt // parts
    return [(slice(b, b + 1), slice(i * step, (i + 1) * step)) for b in range(nb) for i in range(parts)]


def _adaln(x, g, sc, sh):
    ms = jnp.mean(x * x, axis=-1, keepdims=True)
    y = x * lax.rsqrt(ms + EPS) * g
    return y * (1.0 + sc) + sh


def _in_kernel(x_ref, sc_ref, sh_ref, g_ref, w_ref, b_ref, o_ref, h_ref):
    nb, tt, d = x_ref.shape
    j = pl.program_id(2)

    @pl.when(j == 0)
    def _():
        for bs, ts in _row_parts(nb, tt, PROLOGUE_PARTS):
            rows = (bs.stop - bs.start) * (ts.stop - ts.start)
            r0 = bs.start * tt + ts.start
            h = _adaln(x_ref[bs, ts, :], g_ref[...], sc_ref[bs], sh_ref[bs]).reshape(rows, d).astype(BF16)
            h_ref[r0:r0 + rows, :] = h
            z = jnp.dot(h, w_ref[...], preferred_element_type=F32) + b_ref[...]
            o_ref[bs, ts, :] = z.reshape(bs.stop - bs.start, ts.stop - ts.start, -1).astype(o_ref.dtype)

    @pl.when(j > 0)
    def _():
        z = jnp.dot(h_ref[...], w_ref[...], preferred_element_type=F32) + b_ref[...]
        o_ref[...] = z.reshape(nb, tt, -1).astype(o_ref.dtype)


def _in_proj(x, mod3, row0, norm_g, w, b, nb, tt, tn):
    bsz, t, d = x.shape
    n = w.shape[1]
    return pl.pallas_call(
        _in_kernel,
        grid=(bsz // nb, t // tt, n // tn),
        in_specs=[pl.BlockSpec((nb, tt, d), lambda i, s, j: (i, s, 0)),
                  pl.BlockSpec((nb, 1, d), lambda i, s, j: (row0 // nb + i, 0, 1)),
                  pl.BlockSpec((nb, 1, d), lambda i, s, j: (row0 // nb + i, 0, 0)),
                  pl.BlockSpec((1, d), lambda i, s, j: (0, 0)),
                  pl.BlockSpec((d, tn), lambda i, s, j: (0, j)),
                  pl.BlockSpec((1, tn), lambda i, s, j: (0, j))],
        out_specs=pl.BlockSpec((nb, tt, tn), lambda i, s, j: (i, s, j)),
        out_shape=jax.ShapeDtypeStruct((bsz, t, n), BF16),
        scratch_shapes=[pltpu.VMEM((nb * tt, d), BF16)],
        compiler_params=_params(3),
        name="in_proj",
    )(x, mod3, mod3, norm_g, w, b)


CONV_PAD = 32
CONV_STEPS = 16


def _slab_rows(rows):
    pitch = -(-rows // SUBLANES) * SUBLANES
    return pitch if (pitch // SUBLANES) % 2 else pitch + SUBLANES


def _conv_kernel(za_ref, zb_ref, ctx_ref, w_ref, b_ref, lg_ref, lb_ref, a_ref, nc_ref, gbuf, dwbuf):
    nb, tt, c = za_ref.shape
    nct = c // LANES
    kw = w_ref.shape[0]
    tp = gbuf.shape[1] // nct
    tp2 = dwbuf.shape[1] // nct
    lead = CONV_PAD - (kw - 1)
    t = pl.program_id(1)

    @pl.when(t == 0)
    def _():
        for j in range(nct):
            gbuf[:, j * tp:j * tp + CONV_PAD, :] = ctx_ref[:, :, j * LANES:(j + 1) * LANES]

    @pl.when(t > 0)
    def _():
        for j in range(nct):
            gbuf[:, j * tp:j * tp + CONV_PAD, :] = gbuf[:, j * tp + tt:j * tp + tt + CONV_PAD, :]

    for j in range(nct):
        ls = slice(j * LANES, (j + 1) * LANES)
        glu = za_ref[:, :, ls].astype(F32) * _sigmoid(zb_ref[:, :, ls].astype(F32))
        gbuf[:, j * tp + CONV_PAD:j * tp + CONV_PAD + tt, :] = glu
        nc_ref[:, :, ls] = gbuf[:, j * tp + tt:j * tp + tt + CONV_PAD, :]

    w = [w_ref[k] for k in range(kw)]
    bias = b_ref[...]
    nblk = tt // CONV_STEPS

    def body(i, carry):
        n = i // nblk
        t0 = (i % nblk) * CONV_STEPS
        acc = [bias] * CONV_STEPS
        for m in range(CONV_STEPS + kw - 1):
            g = gbuf[n, pl.ds(t0 + lead + m, SUBLANES, stride=tp), :]
            for s in range(max(0, m - (kw - 1)), min(CONV_STEPS - 1, m) + 1):
                acc[s] = acc[s] + g * w[m - s]
        for s in range(CONV_STEPS):
            dwbuf[n, pl.ds(t0 + s, SUBLANES, stride=tp2), :] = acc[s]
        return carry

    lax.fori_loop(0, nb * nblk, body, 0, unroll=2)

    slabs = [dwbuf[:, j * tp2:j * tp2 + tt, :] for j in range(nct)]
    mu = jnp.sum(functools.reduce(jnp.add, slabs), axis=-1, keepdims=True) * (1.0 / c)
    cen = [d - mu for d in slabs]
    var = jnp.sum(functools.reduce(jnp.add, [x * x for x in cen]), axis=-1, keepdims=True) * (1.0 / c)
    inv = lax.rsqrt(var + EPS)
    for j in range(nct):
        ls = slice(j * LANES, (j + 1) * LANES)
        y = cen[j] * inv * lg_ref[:, ls] + lb_ref[:, ls]
        a_ref[:, :, ls] = (y * _sigmoid(y)).astype(a_ref.dtype)


def _conv_branch(z, ctx_pad, conv_w, conv_b, ln_g, ln_b, nb, tt):
    bsz, t, _ = z.shape
    kw, c = conv_w.shape
    nct = c // LANES
    assert nct == SUBLANES and tt % CONV_STEPS == 0
    tp = _slab_rows(CONV_PAD + tt)
    tp2 = _slab_rows(tt)
    return pl.pallas_call(
        _conv_kernel,
        grid=(bsz // nb, t // tt),
        in_specs=[pl.BlockSpec((nb, tt, c), lambda i, s: (i, s, 0)),
                  pl.BlockSpec((nb, tt, c), lambda i, s: (i, s, 1)),
                  pl.BlockSpec((nb, CONV_PAD, c), lambda i, s: (i, 0, 0)),
                  pl.BlockSpec((kw, nct, LANES), lambda i, s: (0, 0, 0)),
                  pl.BlockSpec((nct, LANES), lambda i, s: (0, 0)),
                  pl.BlockSpec((1, c), lambda i, s: (0, 0)),
                  pl.BlockSpec((1, c), lambda i, s: (0, 0))],
        out_specs=[pl.BlockSpec((nb, tt, c), lambda i, s: (i, s, 0)),
                   pl.BlockSpec((nb, CONV_PAD, c), lambda i, s: (i, 0, 0))],
        out_shape=[jax.ShapeDtypeStruct((bsz, t, c), BF16),
                   jax.ShapeDtypeStruct((bsz, CONV_PAD, c), F32)],
        scratch_shapes=[pltpu.VMEM((nb, nct * tp, LANES), F32),
                        pltpu.VMEM((nb, nct * tp2, LANES), F32)],
        compiler_params=_params(2),
        name="conv_branch",
    )(z, z, ctx_pad, conv_w.reshape(kw, nct, LANES), conv_b.reshape(nct, LANES), ln_g, ln_b)


HEAD_BLOCK = 256
ATTN_UNROLL = 4


def _split_dot(x, w, split):
    hi = x.astype(BF16)
    out = jnp.dot(hi, w, preferred_element_type=F32)
    if split:
        lo = (x - hi.astype(F32)).astype(BF16)
        out = out + jnp.dot(lo, w, preferred_element_type=F32)
    return out


def _heads_norm_rope(x, g, cos, sin, mean_w, swap_w, split):
    ms = _split_dot(x * x, mean_w, split)
    y = x * lax.rsqrt(ms + EPS) * g
    return y * cos + _split_dot(y, swap_w, split) * sin


def _attn_kernel(sink_ref, q_ref, k_ref, v_ref, kc_ref, vc_ref, cos_ref, sin_ref, qg_ref, kg_ref, mw_ref, sw_ref,
                 o_ref, nk_ref, nv_ref, kf, vf, kx, vx, qs, s_scr, p_scr, *, mask_prefix):
    tq = q_ref.shape[1]
    kvw = k_ref.shape[2]
    n_kv = kvw // HEAD_DIM
    t = pl.program_id(1)
    reps = HEAD_BLOCK // LANES
    cos = jnp.concatenate([cos_ref[...]] * reps, axis=1)
    sin = jnp.concatenate([sin_ref[...]] * reps, axis=1)

    @pl.when(t == 0)
    def _():
        kf[0:WINDOW, :] = kc_ref[0]
        vf[0:WINDOW, :] = vc_ref[0]

    @pl.when(t > 0)
    def _():
        kf[0:WINDOW, :] = kf[tq:tq + WINDOW, :]
        vf[0:WINDOW, :] = vf[tq:tq + WINDOW, :]

    for j in range(kvw // HEAD_BLOCK):
        ls = slice(j * HEAD_BLOCK, (j + 1) * HEAD_BLOCK)
        kf[WINDOW:WINDOW + tq, ls] = _heads_norm_rope(k_ref[0, :, ls].astype(F32), kg_ref[...], cos, sin,
                                                     mw_ref[...], sw_ref[...], True)
    vf[WINDOW:WINDOW + tq, :] = v_ref[0].astype(F32)
    nk_ref[0] = kf[tq:tq + WINDOW, :]
    nv_ref[0] = vf[tq:tq + WINDOW, :]

    n_buf = WINDOW + tq
    lane = lax.broadcasted_iota(jnp.int32, (n_buf, LANES), 1)
    first = lane < HEAD_DIM
    for src, dst, fill in ((kf, kx, 0.0), (vf, vx, 1.0)):
        for j in range(kvw // LANES):
            tile = src[:, j * LANES:(j + 1) * LANES]
            swapped = pltpu.roll(tile, HEAD_DIM, 1)
            variants = (jnp.where(first, tile, fill), jnp.where(first, fill, swapped),
                        jnp.where(first, swapped, fill), jnp.where(first, fill, tile))
            for m, val in enumerate(variants):
                dst[0:n_buf, (4 * j + m) * LANES:(4 * j + m + 1) * LANES] = val.astype(BF16)
    kx[n_buf:n_buf + CHUNK, :] = jnp.zeros((CHUNK, kx.shape[1]), BF16)

    log2e = 1.4426950408889634
    scale = HEAD_DIM ** -0.5 * log2e
    for j in range(q_ref.shape[2] // HEAD_BLOCK):
        ls = slice(j * HEAD_BLOCK, (j + 1) * HEAD_BLOCK)
        qn = _heads_norm_rope(q_ref[0, :, ls].astype(F32), qg_ref[...], cos, sin,
                              mw_ref[...], sw_ref[...], False)
        qs[:, ls] = (qn * scale).astype(BF16)

    n_keys = WINDOW + CHUNK
    n_cols = n_keys + CHUNK
    n_chunks = tq // CHUNK
    rows = 2 * CHUNK
    dn = (((1,), (1,)), ((), ()))
    lane2 = lax.broadcasted_iota(jnp.int32, (rows, LANES), 1)
    row2 = lax.broadcasted_iota(jnp.int32, (rows, LANES), 0)
    first2 = lane2 < HEAD_DIM

    def score_body(c, carry):
        r0 = pl.multiple_of(c * CHUNK, CHUNK)
        for h in range(n_kv):
            q2 = jnp.concatenate([qs[pl.ds(r0, CHUNK), (2 * h) * LANES:(2 * h + 1) * LANES],
                                  qs[pl.ds(r0, CHUNK), (2 * h + 1) * LANES:(2 * h + 2) * LANES]], axis=0)
            for v in range(2):
                kt = kx[pl.ds(r0, n_cols), (2 * h + v) * LANES:(2 * h + v + 1) * LANES]
                s = lax.dot_general(q2, kt, dn, preferred_element_type=F32)
                sink = jnp.where(row2 < CHUNK, sink_ref[4 * h + v], sink_ref[4 * h + 2 + v]) * log2e
                pad = jnp.where(lane2 == n_keys - LANES, sink, NEG_INF)
                blk = 2 * (c * n_kv + h) + v
                s_scr[blk, :, 0:LANES] = s[:, 0:LANES]
                s_scr[blk, :, LANES:n_cols] = jnp.where(first2, s[:, LANES:n_cols], pad)
        return carry

    lax.fori_loop(0, n_chunks, score_body, 0, unroll=min(ATTN_UNROLL, n_chunks))

    if mask_prefix:
        for c in range(min(2, n_chunks)):
            g = t * n_chunks + c
            blocks = slice(2 * c * n_kv, 2 * (c + 1) * n_kv)

            @pl.when(g == 0)
            def _():
                s_scr[blocks, :, 0:LANES] = jnp.full((2 * n_kv, rows, LANES), NEG_INF, F32)

            @pl.when(g == 1)
            def _():
                s_scr[blocks, :, 0:LANES] = jnp.where(first2[None], NEG_INF, s_scr[blocks, :, 0:LANES])

    s_all = s_scr[...]
    m_all = jnp.max(s_all, axis=-1, keepdims=True)
    p_scr[...] = jnp.exp2(s_all - m_all).astype(BF16)

    e_row = lax.broadcasted_iota(jnp.int32, (CHUNK, LANES), 0) == 0
    e_lane = lax.broadcasted_iota(jnp.int32, (CHUNK, LANES), 1) < HEAD_DIM
    e_lo = jnp.where(jnp.logical_and(e_row, jnp.logical_not(e_lane)), 1.0, 0.0).astype(BF16)
    e_hi = jnp.where(jnp.logical_and(e_row, e_lane), 1.0, 0.0).astype(BF16)

    def out_body(c, carry):
        r0 = pl.multiple_of(c * CHUNK, CHUNK)
        for h in range(n_kv):
            blk = 2 * (c * n_kv + h)
            v_lo = jnp.concatenate([vx[pl.ds(r0, n_keys), (2 * h) * LANES:(2 * h + 1) * LANES], e_lo], axis=0)
            v_hi = jnp.concatenate([vx[pl.ds(r0, n_keys), (2 * h + 1) * LANES:(2 * h + 2) * LANES], e_hi], axis=0)
            o_lo = jnp.dot(p_scr[blk], v_lo, preferred_element_type=F32)
            o_hi = jnp.dot(p_scr[blk + 1], v_hi, preferred_element_type=F32)
            num = jnp.where(first2, o_lo, o_hi)
            den = pltpu.roll(jnp.where(first2, o_hi, o_lo), HEAD_DIM, 1)
            o2 = (num / den).astype(o_ref.dtype)
            o_ref[0, pl.ds(r0, CHUNK), (2 * h) * LANES:(2 * h + 1) * LANES] = o2[0:CHUNK]
            o_ref[0, pl.ds(r0, CHUNK), (2 * h + 1) * LANES:(2 * h + 2) * LANES] = o2[CHUNK:2 * CHUNK]
        return carry

    lax.fori_loop(0, n_chunks, out_body, 0, unroll=min(ATTN_UNROLL, n_chunks))


def _attention(z, k_cache, v_cache, cos, sin, q_g, k_g, sinks, q_col, k_col, aw, kvw, tq, mask_prefix):
    bsz, t, _ = z.shape
    assert q_col % aw == 0 and k_col % kvw == 0 and aw // kvw == 4 and kvw % LANES == 0
    k_blk = k_col // kvw
    n_blocks = 2 * (tq // CHUNK) * (kvw // HEAD_DIM)
    assert kvw % HEAD_BLOCK == 0 and aw % HEAD_BLOCK == 0
    idx = jnp.arange(HEAD_BLOCK)
    mean_w = jnp.where(idx[:, None] // HEAD_DIM == idx[None, :] // HEAD_DIM, 1.0 / HEAD_DIM, 0.0).astype(BF16)
    swap_w = (idx[:, None] == (idx[None, :] ^ (HEAD_DIM // 2))).astype(BF16)
    return pl.pallas_call(
        functools.partial(_attn_kernel, mask_prefix=mask_prefix),
        grid=(bsz, t // tq),
        in_specs=[pl.BlockSpec(memory_space=pltpu.SMEM),
                  pl.BlockSpec((1, tq, aw), lambda i, s: (i, s, q_col // aw)),
                  pl.BlockSpec((1, tq, kvw), lambda i, s: (i, s, k_blk)),
                  pl.BlockSpec((1, tq, kvw), lambda i, s: (i, s, k_blk + 1)),
                  pl.BlockSpec((1, WINDOW, kvw), lambda i, s: (i, 0, 0)),
                  pl.BlockSpec((1, WINDOW, kvw), lambda i, s: (i, 0, 0)),
                  pl.BlockSpec((tq, LANES), lambda i, s: (s, 0)),
                  pl.BlockSpec((tq, LANES), lambda i, s: (s, 0)),
                  pl.BlockSpec((1, HEAD_BLOCK), lambda i, s: (0, 0)),
                  pl.BlockSpec((1, HEAD_BLOCK), lambda i, s: (0, 0)),
                  pl.BlockSpec((HEAD_BLOCK, HEAD_BLOCK), lambda i, s: (0, 0)),
                  pl.BlockSpec((HEAD_BLOCK, HEAD_BLOCK), lambda i, s: (0, 0))],
        out_specs=[pl.BlockSpec((1, tq, aw), lambda i, s: (i, s, 0)),
                   pl.BlockSpec((1, WINDOW, kvw), lambda i, s: (i, 0, 0)),
                   pl.BlockSpec((1, WINDOW, kvw), lambda i, s: (i, 0, 0))],
        scratch_shapes=[pltpu.VMEM((WINDOW + tq, kvw), F32),
                        pltpu.VMEM((WINDOW + tq, kvw), F32),
                        pltpu.VMEM((WINDOW + tq + CHUNK, 4 * kvw), BF16),
                        pltpu.VMEM((WINDOW + tq, 4 * kvw), BF16),
                        pltpu.VMEM((tq, aw), BF16),
                        pltpu.VMEM((n_blocks, 2 * CHUNK, WINDOW + 2 * CHUNK), F32),
                        pltpu.VMEM((n_blocks, 2 * CHUNK, WINDOW + 2 * CHUNK), BF16)],
        out_shape=[jax.ShapeDtypeStruct((bsz, t, aw), BF16),
                   jax.ShapeDtypeStruct((bsz, WINDOW, kvw), F32),
                   jax.ShapeDtypeStruct((bsz, WINDOW, kvw), F32)],
        compiler_params=_params(2),
        name="attention",
    )(sinks, z, z, z, k_cache, v_cache, cos, sin, q_g, k_g, mean_w, swap_w)


MERGE_SUB = 256


def _merge_kernel(a_ref, o_ref, gc_ref, ga_ref, wc_ref, wa_ref, wo_ref, x_ref, g1_ref, y_ref, mg):
    nb, tt, d = x_ref.shape
    tn = wc_ref.shape[1]
    n = pl.program_id(2)
    a = a_ref[...].reshape(nb * tt, -1)
    o = o_ref[...].reshape(nb * tt, -1)
    for c0 in range(0, tn, MERGE_SUB):
        cs = slice(c0, c0 + MERGE_SUB)
        yc = jnp.dot(a, wc_ref[:, cs], preferred_element_type=F32)
        ya = jnp.dot(o, wa_ref[:, cs], preferred_element_type=F32)
        gc = gc_ref[:, :, cs].astype(F32).reshape(nb * tt, MERGE_SUB)
        ga = ga_ref[:, :, cs].astype(F32).reshape(nb * tt, MERGE_SUB)
        merged = _sigmoid(gc) * yc + _sigmoid(ga) * ya
        mg[:, pl.ds(pl.multiple_of(n * tn + c0, MERGE_SUB), MERGE_SUB)] = merged.astype(BF16)

    @pl.when(n == pl.num_programs(2) - 1)
    def _():
        proj = jnp.dot(mg[...], wo_ref[...], preferred_element_type=F32).reshape(nb, tt, d)
        y_ref[...] = x_ref[...] + g1_ref[...] * proj


def _merge(a, o, z, gc_col, ga_col, wc, wa, wo, x, mod3, row0, nb, tt, tn):
    bsz, t, d = x.shape
    c = a.shape[2]
    aw = o.shape[2]
    assert gc_col % tn == 0 and ga_col % tn == 0 and d % tn == 0
    return pl.pallas_call(
        _merge_kernel,
        grid=(bsz // nb, t // tt, d // tn),
        in_specs=[pl.BlockSpec((nb, tt, c), lambda i, s, n: (i, s, 0)),
                  pl.BlockSpec((nb, tt, aw), lambda i, s, n: (i, s, 0)),
                  pl.BlockSpec((nb, tt, tn), lambda i, s, n: (i, s, gc_col // tn + n)),
                  pl.BlockSpec((nb, tt, tn), lambda i, s, n: (i, s, ga_col // tn + n)),
                  pl.BlockSpec((c, tn), lambda i, s, n: (0, n)),
                  pl.BlockSpec((aw, tn), lambda i, s, n: (0, n)),
                  pl.BlockSpec((d, d), lambda i, s, n: (0, 0), pipeline_mode=pl.Buffered(1)),
                  pl.BlockSpec((nb, tt, d), lambda i, s, n: (i, s, 0)),
                  pl.BlockSpec((nb, 1, d), lambda i, s, n: (row0 // nb + i, 0, 2))],
        out_specs=pl.BlockSpec((nb, tt, d), lambda i, s, n: (i, s, 0)),
        out_shape=jax.ShapeDtypeStruct((bsz, t, d), F32),
        scratch_shapes=[pltpu.VMEM((nb * tt, d), BF16)],
        compiler_params=_params(3),
        name="merge_out",
    )(a, o, z, z, wc, wa, wo, x, mod3)


FFN_PAD = SUBLANES
FFN_PARTS = 2


def _ffn_kernel(x_ref, sc_ref, sh_ref, g2_ref, ng_ref, wg_ref, wv_ref, cw_ref, cb_ref, wd_ref, ctx_ref,
                y_ref, nf_ref, h_ref, ubuf, halo):
    nb, tt, d = x_ref.shape
    tf = wg_ref.shape[1]
    kw = cw_ref.shape[0]
    t = pl.program_id(1)
    f = pl.program_id(2)

    @pl.when(t == 0)
    def _():
        ubuf[:, 0:FFN_PAD, :] = ctx_ref[...]

    @pl.when(t > 0)
    def _():
        ubuf[:, 0:FFN_PAD, :] = halo[f]

    def step(first):
        cbias = cb_ref[...].reshape(1, 1, tf)
        taps = [cw_ref[k:k + 1, :].reshape(1, 1, tf) for k in range(kw)]
        for bs, ts in _row_parts(nb, tt, FFN_PARTS):
            nbp, ttp = bs.stop - bs.start, ts.stop - ts.start
            r0 = bs.start * tt + ts.start
            if first:
                hp = _adaln(x_ref[bs, ts, :], ng_ref[...], sc_ref[bs], sh_ref[bs])
                hp = hp.reshape(nbp * ttp, d).astype(BF16)
                h_ref[r0:r0 + nbp * ttp, :] = hp
            else:
                hp = h_ref[r0:r0 + nbp * ttp, :]
            ug = jnp.dot(hp, wg_ref[...], preferred_element_type=F32)
            uv = jnp.dot(hp, wv_ref[...], preferred_element_type=F32)
            ubuf[bs, FFN_PAD + ts.start:FFN_PAD + ts.stop, :] = ug.reshape(nbp, ttp, tf)
            conv = jnp.broadcast_to(cbias, (nbp, ttp, tf))
            for k in range(kw):
                lead = FFN_PAD - (kw - 1) + k + ts.start
                conv = conv + ubuf[bs, lead:lead + ttp, :] * taps[k]
            act = (conv * _sigmoid(conv)).reshape(nbp * ttp, tf) * uv
            part = jnp.dot(act.astype(BF16), wd_ref[...], preferred_element_type=F32).reshape(nbp, ttp, d)
            if first:
                y_ref[bs, ts, :] = part
            else:
                y_ref[bs, ts, :] += part
        tail = ubuf[:, tt:tt + FFN_PAD, :]
        halo[f] = tail
        nf_ref[:, :, pl.ds(pl.multiple_of(f * tf, tf), tf)] = tail

    pl.when(f == 0)(functools.partial(step, True))
    pl.when(f > 0)(functools.partial(step, False))

    @pl.when(f == pl.num_programs(2) - 1)
    def _():
        y_ref[...] = x_ref[...] + g2_ref[...] * y_ref[...]


def _ffn(x, mod3, row0, norm_g, w_up, conv_w, conv_b, w_down, ctx_pad, nb, tt, tf):
    bsz, t, d = x.shape
    dff = w_down.shape[0]
    kw = conv_w.shape[0]
    nf = dff // tf
    assert dff % tf == 0
    return pl.pallas_call(
        _ffn_kernel,
        grid=(bsz // nb, t // tt, nf),
        in_specs=[pl.BlockSpec((nb, tt, d), lambda i, s, f: (i, s, 0)),
                  pl.BlockSpec((nb, 1, d), lambda i, s, f: (row0 // nb + i, 0, 4)),
                  pl.BlockSpec((nb, 1, d), lambda i, s, f: (row0 // nb + i, 0, 3)),
                  pl.BlockSpec((nb, 1, d), lambda i, s, f: (row0 // nb + i, 0, 5)),
                  pl.BlockSpec((1, d), lambda i, s, f: (0, 0)),
                  pl.BlockSpec((d, tf), lambda i, s, f: (0, f)),
                  pl.BlockSpec((d, tf), lambda i, s, f: (0, nf + f)),
                  pl.BlockSpec((kw, tf), lambda i, s, f: (0, f)),
                  pl.BlockSpec((1, tf), lambda i, s, f: (0, f)),
                  pl.BlockSpec((tf, d), lambda i, s, f: (f, 0)),
                  pl.BlockSpec((nb, FFN_PAD, tf), lambda i, s, f: (i, 0, f))],
        out_specs=[pl.BlockSpec((nb, tt, d), lambda i, s, f: (i, s, 0)),
                   pl.BlockSpec((nb, FFN_PAD, dff), lambda i, s, f: (i, 0, 0))],
        out_shape=[jax.ShapeDtypeStruct((bsz, t, d), F32),
                   jax.ShapeDtypeStruct((bsz, FFN_PAD, dff), F32)],
        scratch_shapes=[pltpu.VMEM((nb * tt, d), BF16),
                        pltpu.VMEM((nb, FFN_PAD + tt, tf), F32),
                        pltpu.VMEM((nf, nb, FFN_PAD, tf), F32)],
        compiler_params=_params(3),
        name="conv_ffn",
    )(x, mod3, mod3, mod3, norm_g, w_up, w_up, conv_w, conv_b, w_down, ctx_pad)


def _rope_tables(pos):
    half = HEAD_DIM // 2
    inv_freq = 1.0 / (ROPE_THETA ** (jnp.arange(half, dtype=F32) / half))
    ang = pos.astype(F32)[:, None] * inv_freq[None, :]
    cos = jnp.cos(ang)
    sin = jnp.sin(ang)
    reps = LANES // HEAD_DIM
    return (jnp.tile(jnp.concatenate([cos, cos], axis=-1), (1, reps)),
            jnp.tile(jnp.concatenate([-sin, sin], axis=-1), (1, reps)))


def _front_pad(ctx, rows):
    return jnp.pad(ctx, ((0, 0), (rows - ctx.shape[1], 0), (0, 0)))


def _layer(x, mod3, row0, pos, conv_ctx, k_cache, v_cache, ffn_ctx, mask_prefix, p, tiles):
    bsz, t, d = x.shape
    c = p["conv_w"].shape[1]
    kvw = k_cache.shape[2] * k_cache.shape[3]
    aw = p["attn_o_w"].shape[0]
    nb, tt, tq = tiles["nb"], tiles["tt"], tiles["tq"]
    assert row0 % nb == 0 and bsz % nb == 0 and t % tt == 0 and t % tq == 0
    assert row0 % tiles["nb_merge"] == 0 and bsz % tiles["nb_merge"] == 0

    z = _in_proj(x, mod3, row0, p["norm1_g"], p["w_in"], p["b_in"], nb, tiles["tt_big"], tiles["tn_in"])
    a, nc = _conv_branch(z, _front_pad(conv_ctx, CONV_PAD), p["conv_w"], p["conv_b"], p["ln_g"], p["ln_b"],
                         nb, tiles["tt_big"])
    cos, sin = _rope_tables(pos)
    reps = HEAD_BLOCK // HEAD_DIM
    o, nk, nv = _attention(z, k_cache.reshape(bsz, -1, kvw), v_cache.reshape(bsz, -1, kvw), cos, sin,
                           jnp.tile(p["q_norm_g"], reps)[None, :], jnp.tile(p["k_norm_g"], reps)[None, :],
                           p["sinks"], 2 * c, 2 * c + aw + 2 * d, aw, kvw, tq, mask_prefix)
    gc_col = 2 * c + aw
    x1 = _merge(a, o, z, gc_col, gc_col + d, p["conv_out_w"], p["attn_o_w"], p["w_out"], x, mod3, row0,
                tiles["nb_merge"], tt, tiles["tn_merge"])
    y, nf = _ffn(x1, mod3, row0, p["norm2_g"], p["ffn_up_w"], p["ffn_conv_w"], p["ffn_conv_b"],
                 p["ffn_down_w"], _front_pad(ffn_ctx, FFN_PAD), nb, tiles["tt_big"], tiles["tf"])
    kw = p["conv_w"].shape[0]
    fkw = p["ffn_conv_w"].shape[0]
    return (y, nc[:, CONV_PAD - (kw - 1):], nk.reshape(k_cache.shape), nv.reshape(v_cache.shape),
            nf[:, FFN_PAD - (fkw - 1):])


def kernel(x_prompt, x_sample, c_prompt, c_sample, cache_conv, cache_k, cache_v, cache_ffn_conv, mod_w, mod_b, norm1_g, w_in, b_in, conv_w, conv_b, ln_g, ln_b, conv_out_w, q_norm_g, k_norm_g, sinks, attn_o_w, w_out, norm2_g, ffn_up_w, ffn_conv_w, ffn_conv_b, ffn_down_w):
    depth = mod_w.shape[0]
    bp, tp, d = x_prompt.shape
    bs, ts, _ = x_sample.shape
    pos_p = jnp.arange(tp)
    pos_s = PAST_LEN + jnp.arange(ts)
    yp, ys = x_prompt, x_sample
    outs = [[] for _ in range(8)]
    tiles_p = dict(nb=1, nb_merge=1, tt=min(512, tp), tt_big=min(1024, tp), tq=min(1024, tp), tn_in=1536, tn_merge=1024,
                   tf=512)
    tiles_s = dict(nb=min(16, bs), nb_merge=min(8, bs), tt=ts, tt_big=ts, tq=ts, tn_in=1536, tn_merge=1024,
                   tf=512)
    for l in range(depth):
        kv0 = 2 * conv_w.shape[2] + attn_o_w.shape[1]
        kv1 = kv0 + 2 * cache_k.shape[3] * cache_k.shape[4]
        reorder = lambda m: jnp.concatenate([m[..., :kv0], m[..., kv1:], m[..., kv0:kv1]], axis=-1)
        p = dict(norm1_g=norm1_g[l][None, :], w_in=reorder(w_in[l]).astype(BF16), b_in=reorder(b_in[l])[None, :],
                 conv_w=conv_w[l], conv_b=conv_b[l][None, :], ln_g=ln_g[l][None, :], ln_b=ln_b[l][None, :],
                 conv_out_w=conv_out_w[l].astype(BF16), q_norm_g=q_norm_g[l], k_norm_g=k_norm_g[l],
                 sinks=sinks[l], attn_o_w=attn_o_w[l].astype(BF16), w_out=w_out[l].astype(BF16),
                 norm2_g=norm2_g[l][None, :], ffn_up_w=ffn_up_w[l].astype(BF16), ffn_conv_w=ffn_conv_w[l],
                 ffn_conv_b=ffn_conv_b[l][None, :], ffn_down_w=ffn_down_w[l].astype(BF16))
        c_all = jnp.concatenate([c_prompt, c_sample], axis=0)
        mod3 = _mod(c_all, mod_w[l], mod_b[l][None, :])[:, None, :]
        zeros_conv = jnp.zeros((bp,) + cache_conv.shape[2:], F32)
        zeros_kv = jnp.zeros((bp,) + cache_k.shape[2:], F32)
        zeros_ffn = jnp.zeros((bp,) + cache_ffn_conv.shape[2:], F32)
        yp, nc_p, nk_p, nv_p, nf_p = _layer(yp, mod3, 0, pos_p, zeros_conv, zeros_kv, zeros_kv, zeros_ffn,
                                            True, p, tiles_p)
        ys, nc_s, nk_s, nv_s, nf_s = _layer(ys, mod3, bp, pos_s, cache_conv[l], cache_k[l], cache_v[l],
                                            cache_ffn_conv[l], False, p, tiles_s)
        for lst, val in zip(outs, (nc_p, nc_s, nk_p, nk_s, nv_p, nv_s, nf_p, nf_s)):
            lst.append(val)
    return (yp, ys) + tuple(jnp.stack(o) for o in outs)
```

```python
import functools

import jax
import jax.numpy as jnp
from jax import lax
from jax.experimental import pallas as pl
from jax.experimental.pallas import tpu as pltpu

CHUNK = 64
HEAD_DIM = 64
WINDOW = 128
PAST_LEN = 1024
ROPE_THETA = 10000.0
EPS = 1e-6
NEG_INF = -1e30
LANES = 128
SUBLANES = 8
VMEM_LIMIT_BYTES = 60 * 1024 * 1024

F32 = jnp.float32
BF16 = jnp.bfloat16


def _params(n_axes):
    return pltpu.CompilerParams(dimension_semantics=("arbitrary",) * n_axes,
                                vmem_limit_bytes=VMEM_LIMIT_BYTES)


def _sigmoid(x):
    return 1.0 / (1.0 + jnp.exp(-x))


def _mod_kernel(c_ref, w_ref, b_ref, o_ref):
    c = c_ref[...]
    a = (c * _sigmoid(c)).astype(BF16)
    o_ref[...] = jnp.dot(a, w_ref[...].astype(BF16), preferred_element_type=F32) + b_ref[...]


def _mod(c_all, mod_w, mod_b, tn=1024):
    m, d = c_all.shape
    n = mod_w.shape[1]
    return pl.pallas_call(
        _mod_kernel,
        grid=(n // tn,),
        in_specs=[pl.BlockSpec((m, d), lambda j: (0, 0)),
                  pl.BlockSpec((d, tn), lambda j: (0, j)),
                  pl.BlockSpec((1, tn), lambda j: (0, j))],
        out_specs=pl.BlockSpec((m, tn), lambda j: (0, j)),
        out_shape=jax.ShapeDtypeStruct((m, n), F32),
        compiler_params=_params(1),
        name="mod",
    )(c_all, mod_w, mod_b)


PROLOGUE_PARTS = 4


def _row_parts(nb, tt, parts):
    if nb >= parts:
        step = nb // parts
        return [(slice(i * step, (i + 1) * step), slice(0, tt)) for i in range(parts)]
    step = tt // parts
    return [(slice(b, b + 1), slice(i * step, (i + 1) * step)) for b in range(nb) for i in range(parts)]


def _adaln(x, g, sc, sh):
    ms = jnp.mean(x * x, axis=-1, keepdims=True)
    y = x * lax.rsqrt(ms + EPS) * g
    return y * (1.0 + sc) + sh


def _in_kernel(x_ref, sc_ref, sh_ref, g_ref, w_ref, b_ref, o_ref, h_ref):
    nb, tt, d = x_ref.shape
    j = pl.program_id(2)

    @pl.when(j == 0)
    def _():
        for bs, ts in _row_parts(nb, tt, PROLOGUE_PARTS):
            rows = (bs.stop - bs.start) * (ts.stop - ts.start)
            r0 = bs.start * tt + ts.start
            h = _adaln(x_ref[bs, ts, :], g_ref[...], sc_ref[bs], sh_ref[bs]).reshape(rows, d).astype(BF16)
            h_ref[r0:r0 + rows, :] = h
            z = jnp.dot(h, w_ref[...], preferred_element_type=F32) + b_ref[...]
            o_ref[bs, ts, :] = z.reshape(bs.stop - bs.start, ts.stop - ts.start, -1).astype(o_ref.dtype)

    @pl.when(j > 0)
    def _():
        z = jnp.dot(h_ref[...], w_ref[...], preferred_element_type=F32) + b_ref[...]
        o_ref[...] = z.reshape(nb, tt, -1).astype(o_ref.dtype)


def _in_proj(x, mod3, row0, norm_g, w, b, nb, tt, tn):
    bsz, t, d = x.shape
    n = w.shape[1]
    return pl.pallas_call(
        _in_kernel,
        grid=(bsz // nb, t // tt, n // tn),
        in_specs=[pl.BlockSpec((nb, tt, d), lambda i, s, j: (i, s, 0)),
                  pl.BlockSpec((nb, 1, d), lambda i, s, j: (row0 // nb + i, 0, 1)),
                  pl.BlockSpec((nb, 1, d), lambda i, s, j: (row0 // nb + i, 0, 0)),
                  pl.BlockSpec((1, d), lambda i, s, j: (0, 0)),
                  pl.BlockSpec((d, tn), lambda i, s, j: (0, j)),
                  pl.BlockSpec((1, tn), lambda i, s, j: (0, j))],
        out_specs=pl.BlockSpec((nb, tt, tn), lambda i, s, j: (i, s, j)),
        out_shape=jax.ShapeDtypeStruct((bsz, t, n), BF16),
        scratch_shapes=[pltpu.VMEM((nb * tt, d), BF16)],
        compiler_params=_params(3),
        name="in_proj",
    )(x, mod3, mod3, norm_g, w, b)


CONV_PAD = 32
CONV_STEPS = 16


def _slab_rows(rows):
    pitch = -(-rows // SUBLANES) * SUBLANES
    return pitch if (pitch // SUBLANES) % 2 else pitch + SUBLANES


def _conv_kernel(za_ref, zb_ref, ctx_ref, w_ref, b_ref, lg_ref, lb_ref, a_ref, nc_ref, gbuf, dwbuf):
    nb, tt, c = za_ref.shape
    nct = c // LANES
    kw = w_ref.shape[0]
    tp = gbuf.shape[1] // nct
    tp2 = dwbuf.shape[1] // nct
    lead = CONV_PAD - (kw - 1)
    t = pl.program_id(1)

    @pl.when(t == 0)
    def _():
        for j in range(nct):
            gbuf[:, j * tp:j * tp + CONV_PAD, :] = ctx_ref[:, :, j * LANES:(j + 1) * LANES]

    @pl.when(t > 0)
    def _():
        for j in range(nct):
            gbuf[:, j * tp:j * tp + CONV_PAD, :] = gbuf[:, j * tp + tt:j * tp + tt + CONV_PAD, :]

    for j in range(nct):
        ls = slice(j * LANES, (j + 1) * LANES)
        glu = za_ref[:, :, ls].astype(F32) * _sigmoid(zb_ref[:, :, ls].astype(F32))
        gbuf[:, j * tp + CONV_PAD:j * tp + CONV_PAD + tt, :] = glu
        nc_ref[:, :, ls] = gbuf[:, j * tp + tt:j * tp + tt + CONV_PAD, :]

    w = [w_ref[k] for k in range(kw)]
    bias = b_ref[...]
    nblk = tt // CONV_STEPS

    def body(i, carry):
        n = i // nblk
        t0 = (i % nblk) * CONV_STEPS
        acc = [bias] * CONV_STEPS
        for m in range(CONV_STEPS + kw - 1):
            g = gbuf[n, pl.ds(t0 + lead + m, SUBLANES, stride=tp), :]
            for s in range(max(0, m - (kw - 1)), min(CONV_STEPS - 1, m) + 1):
                acc[s] = acc[s] + g * w[m - s]
        for s in range(CONV_STEPS):
            dwbuf[n, pl.ds(t0 + s, SUBLANES, stride=tp2), :] = acc[s]
        return carry

    lax.fori_loop(0, nb * nblk, body, 0, unroll=2)

    slabs = [dwbuf[:, j * tp2:j * tp2 + tt, :] for j in range(nct)]
    mu = jnp.sum(functools.reduce(jnp.add, slabs), axis=-1, keepdims=True) * (1.0 / c)
    cen = [d - mu for d in slabs]
    var = jnp.sum(functools.reduce(jnp.add, [x * x for x in cen]), axis=-1, keepdims=True) * (1.0 / c)
    inv = lax.rsqrt(var + EPS)
    for j in range(nct):
        ls = slice(j * LANES, (j + 1) * LANES)
        y = cen[j] * inv * lg_ref[:, ls] + lb_ref[:, ls]
        a_ref[:, :, ls] = (y * _sigmoid(y)).astype(a_ref.dtype)


def _conv_branch(z, ctx_pad, conv_w, conv_b, ln_g, ln_b, nb, tt):
    bsz, t, _ = z.shape
    kw, c = conv_w.shape
    nct = c // LANES
    assert nct == SUBLANES and tt % CONV_STEPS == 0
    tp = _slab_rows(CONV_PAD + tt)
    tp2 = _slab_rows(tt)
    return pl.pallas_call(
        _conv_kernel,
        grid=(bsz // nb, t // tt),
        in_specs=[pl.BlockSpec((nb, tt, c), lambda i, s: (i, s, 0)),
                  pl.BlockSpec((nb, tt, c), lambda i, s: (i, s, 1)),
                  pl.BlockSpec((nb, CONV_PAD, c), lambda i, s: (i, 0, 0)),
                  pl.BlockSpec((kw, nct, LANES), lambda i, s: (0, 0, 0)),
                  pl.BlockSpec((nct, LANES), lambda i, s: (0, 0)),
                  pl.BlockSpec((1, c), lambda i, s: (0, 0)),
                  pl.BlockSpec((1, c), lambda i, s: (0, 0))],
        out_specs=[pl.BlockSpec((nb, tt, c), lambda i, s: (i, s, 0)),
                   pl.BlockSpec((nb, CONV_PAD, c), lambda i, s: (i, 0, 0))],
        out_shape=[jax.ShapeDtypeStruct((bsz, t, c), BF16),
                   jax.ShapeDtypeStruct((bsz, CONV_PAD, c), F32)],
        scratch_shapes=[pltpu.VMEM((nb, nct * tp, LANES), F32),
                        pltpu.VMEM((nb, nct * tp2, LANES), F32)],
        compiler_params=_params(2),
        name="conv_branch",
    )(z, z, ctx_pad, conv_w.reshape(kw, nct, LANES), conv_b.reshape(nct, LANES), ln_g, ln_b)


HEAD_BLOCK = 256
ATTN_UNROLL = 4


def _split_dot(x, w, split):
    hi = x.astype(BF16)
    out = jnp.dot(hi, w, preferred_element_type=F32)
    if split:
        lo = (x - hi.astype(F32)).astype(BF16)
        out = out + jnp.dot(lo, w, preferred_element_type=F32)
    return out


def _heads_norm_rope(x, g, cos, sin, mean_w, swap_w, split):
    ms = _split_dot(x * x, mean_w, split)
    y = x * lax.rsqrt(ms + EPS) * g
    return y * cos + _split_dot(y, swap_w, split) * sin


def _attn_kernel(sink_ref, q_ref, k_ref, v_ref, kc_ref, vc_ref, cos_ref, sin_ref, qg_ref, kg_ref, mw_ref, sw_ref,
                 o_ref, nk_ref, nv_ref, kf, vf, kx, vx, qs, s_scr, p_scr, *, mask_prefix):
    tq = q_ref.shape[1]
    kvw = k_ref.shape[2]
    n_kv = kvw // HEAD_DIM
    t = pl.program_id(1)
    reps = HEAD_BLOCK // LANES
    cos = jnp.concatenate([cos_ref[...]] * reps, axis=1)
    sin = jnp.concatenate([sin_ref[...]] * reps, axis=1)

    @pl.when(t == 0)
    def _():
        kf[0:WINDOW, :] = kc_ref[0]
        vf[0:WINDOW, :] = vc_ref[0]

    @pl.when(t > 0)
    def _():
        kf[0:WINDOW, :] = kf[tq:tq + WINDOW, :]
        vf[0:WINDOW, :] = vf[tq:tq + WINDOW, :]

    for j in range(kvw // HEAD_BLOCK):
        ls = slice(j * HEAD_BLOCK, (j + 1) * HEAD_BLOCK)
        kf[WINDOW:WINDOW + tq, ls] = _heads_norm_rope(k_ref[0, :, ls].astype(F32), kg_ref[...], cos, sin,
                                                     mw_ref[...], sw_ref[...], True)
    vf[WINDOW:WINDOW + tq, :] = v_ref[0].astype(F32)
    nk_ref[0] = kf[tq:tq + WINDOW, :]
    nv_ref[0] = vf[tq:tq + WINDOW, :]

    n_buf = WINDOW + tq
    lane = lax.broadcasted_iota(jnp.int32, (n_buf, LANES), 1)
    first = lane < HEAD_DIM
    for src, dst, fill in ((kf, kx, 0.0), (vf, vx, 1.0)):
        for j in range(kvw // LANES):
            tile = src[:, j * LANES:(j + 1) * LANES]
            swapped = pltpu.roll(tile, HEAD_DIM, 1)
            variants = (jnp.where(first, tile, fill), jnp.where(first, fill, swapped),
                        jnp.where(first, swapped, fill), jnp.where(first, fill, tile))
            for m, val in enumerate(variants):
                dst[0:n_buf, (4 * j + m) * LANES:(4 * j + m + 1) * LANES] = val.astype(BF16)
    kx[n_buf:n_buf + CHUNK, :] = jnp.zeros((CHUNK, kx.shape[1]), BF16)

    log2e = 1.4426950408889634
    scale = HEAD_DIM ** -0.5 * log2e
    for j in range(q_ref.shape[2] // HEAD_BLOCK):
        ls = slice(j * HEAD_BLOCK, (j + 1) * HEAD_BLOCK)
        qn = _heads_norm_rope(q_ref[0, :, ls].astype(F32), qg_ref[...], cos, sin,
                              mw_ref[...], sw_ref[...], False)
        qs[:, ls] = (qn * scale).astype(BF16)

    n_keys = WINDOW + CHUNK
    n_cols = n_keys + CHUNK
    n_chunks = tq // CHUNK
    rows = 2 * CHUNK
    dn = (((1,), (1,)), ((), ()))
    lane2 = lax.broadcasted_iota(jnp.int32, (rows, LANES), 1)
    row2 = lax.broadcasted_iota(jnp.int32, (rows, LANES), 0)
    first2 = lane2 < HEAD_DIM

    def score_body(c, carry):
        r0 = pl.multiple_of(c * CHUNK, CHUNK)
        for h in range(n_kv):
            q2 = jnp.concatenate([qs[pl.ds(r0, CHUNK), (2 * h) * LANES:(2 * h + 1) * LANES],
                                  qs[pl.ds(r0, CHUNK), (2 * h + 1) * LANES:(2 * h + 2) * LANES]], axis=0)
            for v in range(2):
                kt = kx[pl.ds(r0, n_cols), (2 * h + v) * LANES:(2 * h + v + 1) * LANES]
                s = lax.dot_general(q2, kt, dn, preferred_element_type=F32)
                sink = jnp.where(row2 < CHUNK, sink_ref[4 * h + v], sink_ref[4 * h + 2 + v]) * log2e
                pad = jnp.where(lane2 == n_keys - LANES, sink, NEG_INF)
                blk = 2 * (c * n_kv + h) + v
                s_scr[blk, :, 0:LANES] = s[:, 0:LANES]
                s_scr[blk, :, LANES:n_cols] = jnp.where(first2, s[:, LANES:n_cols], pad)
        return carry

    lax.fori_loop(0, n_chunks, score_body, 0, unroll=min(ATTN_UNROLL, n_chunks))

    if mask_prefix:
        for c in range(min(2, n_chunks)):
            g = t * n_chunks + c
            blocks = slice(2 * c * n_kv, 2 * (c + 1) * n_kv)

            @pl.when(g == 0)
            def _():
                s_scr[blocks, :, 0:LANES] = jnp.full((2 * n_kv, rows, LANES), NEG_INF, F32)

            @pl.when(g == 1)
            def _():
                s_scr[blocks, :, 0:LANES] = jnp.where(first2[None], NEG_INF, s_scr[blocks, :, 0:LANES])

    s_all = s_scr[...]
    m_all = jnp.max(s_all, axis=-1, keepdims=True)
    p_scr[...] = jnp.exp2(s_all - m_all).astype(BF16)

    e_row = lax.broadcasted_iota(jnp.int32, (CHUNK, LANES), 0) == 0
    e_lane = lax.broadcasted_iota(jnp.int32, (CHUNK, LANES), 1) < HEAD_DIM
    e_lo = jnp.where(jnp.logical_and(e_row, jnp.logical_not(e_lane)), 1.0, 0.0).astype(BF16)
    e_hi = jnp.where(jnp.logical_and(e_row, e_lane), 1.0, 0.0).astype(BF16)

    def out_body(c, carry):
        r0 = pl.multiple_of(c * CHUNK, CHUNK)
        for h in range(n_kv):
            blk = 2 * (c * n_kv + h)
            v_lo = jnp.concatenate([vx[pl.ds(r0, n_keys), (2 * h) * LANES:(2 * h + 1) * LANES], e_lo], axis=0)
            v_hi = jnp.concatenate([vx[pl.ds(r0, n_keys), (2 * h + 1) * LANES:(2 * h + 2) * LANES], e_hi], axis=0)
            o_lo = jnp.dot(p_scr[blk], v_lo, preferred_element_type=F32)
            o_hi = jnp.dot(p_scr[blk + 1], v_hi, preferred_element_type=F32)
            num = jnp.where(first2, o_lo, o_hi)
            den = pltpu.roll(jnp.where(first2, o_hi, o_lo), HEAD_DIM, 1)
            o2 = (num / den).astype(o_ref.dtype)
            o_ref[0, pl.ds(r0, CHUNK), (2 * h) * LANES:(2 * h + 1) * LANES] = o2[0:CHUNK]
            o_ref[0, pl.ds(r0, CHUNK), (2 * h + 1) * LANES:(2 * h + 2) * LANES] = o2[CHUNK:2 * CHUNK]
        return carry

    lax.fori_loop(0, n_chunks, out_body, 0, unroll=min(ATTN_UNROLL, n_chunks))


def _attention(z, k_cache, v_cache, cos, sin, q_g, k_g, sinks, q_col, k_col, aw, kvw, tq, mask_prefix):
    bsz, t, _ = z.shape
    assert q_col % aw == 0 and k_col % kvw == 0 and aw // kvw == 4 and kvw % LANES == 0
    k_blk = k_col // kvw
    n_blocks = 2 * (tq // CHUNK) * (kvw // HEAD_DIM)
    assert kvw % HEAD_BLOCK == 0 and aw % HEAD_BLOCK == 0
    idx = jnp.arange(HEAD_BLOCK)
    mean_w = jnp.where(idx[:, None] // HEAD_DIM == idx[None, :] // HEAD_DIM, 1.0 / HEAD_DIM, 0.0).astype(BF16)
    swap_w = (idx[:, None] == (idx[None, :] ^ (HEAD_DIM // 2))).astype(BF16)
    return pl.pallas_call(
        functools.partial(_attn_kernel, mask_prefix=mask_prefix),
        grid=(bsz, t // tq),
        in_specs=[pl.BlockSpec(memory_space=pltpu.SMEM),
                  pl.BlockSpec((1, tq, aw), lambda i, s: (i, s, q_col // aw)),
                  pl.BlockSpec((1, tq, kvw), lambda i, s: (i, s, k_blk)),
                  pl.BlockSpec((1, tq, kvw), lambda i, s: (i, s, k_blk + 1)),
                  pl.BlockSpec((1, WINDOW, kvw), lambda i, s: (i, 0, 0)),
                  pl.BlockSpec((1, WINDOW, kvw), lambda i, s: (i, 0, 0)),
                  pl.BlockSpec((tq, LANES), lambda i, s: (s, 0)),
                  pl.BlockSpec((tq, LANES), lambda i, s: (s, 0)),
                  pl.BlockSpec((1, HEAD_BLOCK), lambda i, s: (0, 0)),
                  pl.BlockSpec((1, HEAD_BLOCK), lambda i, s: (0, 0)),
                  pl.BlockSpec((HEAD_BLOCK, HEAD_BLOCK), lambda i, s: (0, 0)),
                  pl.BlockSpec((HEAD_BLOCK, HEAD_BLOCK), lambda i, s: (0, 0))],
        out_specs=[pl.BlockSpec((1, tq, aw), lambda i, s: (i, s, 0)),
                   pl.BlockSpec((1, WINDOW, kvw), lambda i, s: (i, 0, 0)),
                   pl.BlockSpec((1, WINDOW, kvw), lambda i, s: (i, 0, 0))],
        scratch_shapes=[pltpu.VMEM((WINDOW + tq, kvw), F32),
                        pltpu.VMEM((WINDOW + tq, kvw), F32),
                        pltpu.VMEM((WINDOW + tq + CHUNK, 4 * kvw), BF16),
                        pltpu.VMEM((WINDOW + tq, 4 * kvw), BF16),
                        pltpu.VMEM((tq, aw), BF16),
                        pltpu.VMEM((n_blocks, 2 * CHUNK, WINDOW + 2 * CHUNK), F32),
                        pltpu.VMEM((n_blocks, 2 * CHUNK, WINDOW + 2 * CHUNK), BF16)],
        out_shape=[jax.ShapeDtypeStruct((bsz, t, aw), BF16),
                   jax.ShapeDtypeStruct((bsz, WINDOW, kvw), F32),
                   jax.ShapeDtypeStruct((bsz, WINDOW, kvw), F32)],
        compiler_params=_params(2),
        name="attention",
    )(sinks, z, z, z, k_cache, v_cache, cos, sin, q_g, k_g, mean_w, swap_w)


MERGE_SUB = 256


def _merge_kernel(a_ref, o_ref, gc_ref, ga_ref, wc_ref, wa_ref, wo_ref, x_ref, g1_ref, y_ref, mg):
    nb, tt, d = x_ref.shape
    tn = gc_ref.shape[2]
    n = pl.program_id(2)
    a = a_ref[...].reshape(nb * tt, -1)
    o = o_ref[...].reshape(nb * tt, -1)
    for c0 in range(0, tn, MERGE_SUB):
        cs = slice(c0, c0 + MERGE_SUB)
        ws = pl.ds(pl.multiple_of(n * tn + c0, MERGE_SUB), MERGE_SUB)
        yc = jnp.dot(a, wc_ref[:, ws], preferred_element_type=F32)
        ya = jnp.dot(o, wa_ref[:, ws], preferred_element_type=F32)
        gc = gc_ref[:, :, cs].astype(F32).reshape(nb * tt, MERGE_SUB)
        ga = ga_ref[:, :, cs].astype(F32).reshape(nb * tt, MERGE_SUB)
        merged = _sigmoid(gc) * yc + _sigmoid(ga) * ya
        mg[:, ws] = merged.astype(BF16)

    @pl.when(n == pl.num_programs(2) - 1)
    def _():
        proj = jnp.dot(mg[...], wo_ref[...], preferred_element_type=F32).reshape(nb, tt, d)
        y_ref[...] = x_ref[...] + g1_ref[...] * proj


def _merge(a, o, z, gc_col, ga_col, wc, wa, wo, x, mod3, row0, nb, tt, tn):
    bsz, t, d = x.shape
    c = a.shape[2]
    aw = o.shape[2]
    assert gc_col % tn == 0 and ga_col % tn == 0 and d % tn == 0
    return pl.pallas_call(
        _merge_kernel,
        grid=(bsz // nb, t // tt, d // tn),
        in_specs=[pl.BlockSpec((nb, tt, c), lambda i, s, n: (i, s, 0)),
                  pl.BlockSpec((nb, tt, aw), lambda i, s, n: (i, s, 0)),
                  pl.BlockSpec((nb, tt, tn), lambda i, s, n: (i, s, gc_col // tn + n)),
                  pl.BlockSpec((nb, tt, tn), lambda i, s, n: (i, s, ga_col // tn + n)),
                  pl.BlockSpec((c, d), lambda i, s, n: (0, 0), pipeline_mode=pl.Buffered(1)),
                  pl.BlockSpec((aw, d), lambda i, s, n: (0, 0), pipeline_mode=pl.Buffered(1)),
                  pl.BlockSpec((d, d), lambda i, s, n: (0, 0), pipeline_mode=pl.Buffered(1)),
                  pl.BlockSpec((nb, tt, d), lambda i, s, n: (i, s, 0)),
                  pl.BlockSpec((nb, 1, d), lambda i, s, n: (row0 // nb + i, 0, 2))],
        out_specs=pl.BlockSpec((nb, tt, d), lambda i, s, n: (i, s, 0)),
        out_shape=jax.ShapeDtypeStruct((bsz, t, d), F32),
        scratch_shapes=[pltpu.VMEM((nb * tt, d), BF16)],
        compiler_params=_params(3),
        name="merge_out",
    )(a, o, z, z, wc, wa, wo, x, mod3)


FFN_PAD = SUBLANES
FFN_PARTS = 2


def _ffn_kernel(x_ref, sc_ref, sh_ref, g2_ref, ng_ref, wg_ref, wv_ref, cw_ref, cb_ref, wd_ref, ctx_ref,
                y_ref, nf_ref, h_ref, ubuf, halo):
    nb, tt, d = x_ref.shape
    tf = wg_ref.shape[1]
    kw = cw_ref.shape[0]
    t = pl.program_id(1)
    f = pl.program_id(2)

    @pl.when(t == 0)
    def _():
        ubuf[:, 0:FFN_PAD, :] = ctx_ref[...]

    @pl.when(t > 0)
    def _():
        ubuf[:, 0:FFN_PAD, :] = halo[f]

    def step(first):
        cbias = cb_ref[...].reshape(1, 1, tf)
        taps = [cw_ref[k:k + 1, :].reshape(1, 1, tf) for k in range(kw)]
        for bs, ts in _row_parts(nb, tt, FFN_PARTS):
            nbp, ttp = bs.stop - bs.start, ts.stop - ts.start
            r0 = bs.start * tt + ts.start
            if first:
                hp = _adaln(x_ref[bs, ts, :], ng_ref[...], sc_ref[bs], sh_ref[bs])
                hp = hp.reshape(nbp * ttp, d).astype(BF16)
                h_ref[r0:r0 + nbp * ttp, :] = hp
            else:
                hp = h_ref[r0:r0 + nbp * ttp, :]
            ug = jnp.dot(hp, wg_ref[...], preferred_element_type=F32)
            uv = jnp.dot(hp, wv_ref[...], preferred_element_type=F32)
            ubuf[bs, FFN_PAD + ts.start:FFN_PAD + ts.stop, :] = ug.reshape(nbp, ttp, tf)
            conv = jnp.broadcast_to(cbias, (nbp, ttp, tf))
            for k in range(kw):
                lead = FFN_PAD - (kw - 1) + k + ts.start
                conv = conv + ubuf[bs, lead:lead + ttp, :] * taps[k]
            act = (conv * _sigmoid(conv)).reshape(nbp * ttp, tf) * uv
            part = jnp.dot(act.astype(BF16), wd_ref[...], preferred_element_type=F32).reshape(nbp, ttp, d)
            if first:
                y_ref[bs, ts, :] = part
            else:
                y_ref[bs, ts, :] += part
        tail = ubuf[:, tt:tt + FFN_PAD, :]
        halo[f] = tail
        nf_ref[:, :, pl.ds(pl.multiple_of(f * tf, tf), tf)] = tail

    pl.when(f == 0)(functools.partial(step, True))
    pl.when(f > 0)(functools.partial(step, False))

    @pl.when(f == pl.num_programs(2) - 1)
    def _():
        y_ref[...] = x_ref[...] + g2_ref[...] * y_ref[...]


def _ffn(x, mod3, row0, norm_g, w_up, conv_w, conv_b, w_down, ctx_pad, nb, tt, tf):
    bsz, t, d = x.shape
    dff = w_down.shape[0]
    kw = conv_w.shape[0]
    nf = dff // tf
    assert dff % tf == 0
    return pl.pallas_call(
        _ffn_kernel,
        grid=(bsz // nb, t // tt, nf),
        in_specs=[pl.BlockSpec((nb, tt, d), lambda i, s, f: (i, s, 0)),
                  pl.BlockSpec((nb, 1, d), lambda i, s, f: (row0 // nb + i, 0, 4)),
                  pl.BlockSpec((nb, 1, d), lambda i, s, f: (row0 // nb + i, 0, 3)),
                  pl.BlockSpec((nb, 1, d), lambda i, s, f: (row0 // nb + i, 0, 5)),
                  pl.BlockSpec((1, d), lambda i, s, f: (0, 0)),
                  pl.BlockSpec((d, tf), lambda i, s, f: (0, f)),
                  pl.BlockSpec((d, tf), lambda i, s, f: (0, nf + f)),
                  pl.BlockSpec((kw, tf), lambda i, s, f: (0, f)),
                  pl.BlockSpec((1, tf), lambda i, s, f: (0, f)),
                  pl.BlockSpec((tf, d), lambda i, s, f: (f, 0)),
                  pl.BlockSpec((nb, FFN_PAD, tf), lambda i, s, f: (i, 0, f))],
        out_specs=[pl.BlockSpec((nb, tt, d), lambda i, s, f: (i, s, 0)),
                   pl.BlockSpec((nb, FFN_PAD, dff), lambda i, s, f: (i, 0, 0))],
        out_shape=[jax.ShapeDtypeStruct((bsz, t, d), F32),
                   jax.ShapeDtypeStruct((bsz, FFN_PAD, dff), F32)],
        scratch_shapes=[pltpu.VMEM((nb * tt, d), BF16),
                        pltpu.VMEM((nb, FFN_PAD + tt, tf), F32),
                        pltpu.VMEM((nf, nb, FFN_PAD, tf), F32)],
        compiler_params=_params(3),
        name="conv_ffn",
    )(x, mod3, mod3, mod3, norm_g, w_up, w_up, conv_w, conv_b, w_down, ctx_pad)


def _rope_tables(pos):
    half = HEAD_DIM // 2
    inv_freq = 1.0 / (ROPE_THETA ** (jnp.arange(half, dtype=F32) / half))
    ang = pos.astype(F32)[:, None] * inv_freq[None, :]
    cos = jnp.cos(ang)
    sin = jnp.sin(ang)
    reps = LANES // HEAD_DIM
    return (jnp.tile(jnp.concatenate([cos, cos], axis=-1), (1, reps)),
            jnp.tile(jnp.concatenate([-sin, sin], axis=-1), (1, reps)))


def _front_pad(ctx, rows):
    return jnp.pad(ctx, ((0, 0), (rows - ctx.shape[1], 0), (0, 0)))


def _layer(x, mod3, row0, pos, conv_ctx, k_cache, v_cache, ffn_ctx, mask_prefix, p, tiles):
    bsz, t, d = x.shape
    c = p["conv_w"].shape[1]
    kvw = k_cache.shape[2] * k_cache.shape[3]
    aw = p["attn_o_w"].shape[0]
    nb, tt, tq = tiles["nb"], tiles["tt"], tiles["tq"]
    assert row0 % nb == 0 and bsz % nb == 0 and t % tt == 0 and t % tq == 0
    assert row0 % tiles["nb_merge"] == 0 and bsz % tiles["nb_merge"] == 0

    z = _in_proj(x, mod3, row0, p["norm1_g"], p["w_in"], p["b_in"], nb, tiles["tt_big"], tiles["tn_in"])
    a, nc = _conv_branch(z, _front_pad(conv_ctx, CONV_PAD), p["conv_w"], p["conv_b"], p["ln_g"], p["ln_b"],
                         nb, tiles["tt_big"])
    cos, sin = _rope_tables(pos)
    reps = HEAD_BLOCK // HEAD_DIM
    o, nk, nv = _attention(z, k_cache.reshape(bsz, -1, kvw), v_cache.reshape(bsz, -1, kvw), cos, sin,
                           jnp.tile(p["q_norm_g"], reps)[None, :], jnp.tile(p["k_norm_g"], reps)[None, :],
                           p["sinks"], 2 * c, 2 * c + aw + 2 * d, aw, kvw, tq, mask_prefix)
    gc_col = 2 * c + aw
    x1 = _merge(a, o, z, gc_col, gc_col + d, p["conv_out_w"], p["attn_o_w"], p["w_out"], x, mod3, row0,
                tiles["nb_merge"], tt, tiles["tn_merge"])
    y, nf = _ffn(x1, mod3, row0, p["norm2_g"], p["ffn_up_w"], p["ffn_conv_w"], p["ffn_conv_b"],
                 p["ffn_down_w"], _front_pad(ffn_ctx, FFN_PAD), nb, tiles["tt_big"], tiles["tf"])
    kw = p["conv_w"].shape[0]
    fkw = p["ffn_conv_w"].shape[0]
    return (y, nc[:, CONV_PAD - (kw - 1):], nk.reshape(k_cache.shape), nv.reshape(v_cache.shape),
            nf[:, FFN_PAD - (fkw - 1):])


def kernel(x_prompt, x_sample, c_prompt, c_sample, cache_conv, cache_k, cache_v, cache_ffn_conv, mod_w, mod_b, norm1_g, w_in, b_in, conv_w, conv_b, ln_g, ln_b, conv_out_w, q_norm_g, k_norm_g, sinks, attn_o_w, w_out, norm2_g, ffn_up_w, ffn_conv_w, ffn_conv_b, ffn_down_w):
    depth = mod_w.shape[0]
    bp, tp, d = x_prompt.shape
    bs, ts, _ = x_sample.shape
    pos_p = jnp.arange(tp)
    pos_s = PAST_LEN + jnp.arange(ts)
    yp, ys = x_prompt, x_sample
    outs = [[] for _ in range(8)]
    tiles_p = dict(nb=1, nb_merge=1, tt=min(512, tp), tt_big=min(1024, tp), tq=min(1024, tp), tn_in=1536, tn_merge=1024,
                   tf=512)
    tiles_s = dict(nb=min(16, bs), nb_merge=min(8, bs), tt=ts, tt_big=ts, tq=ts, tn_in=1536, tn_merge=1024,
                   tf=512)
    for l in range(depth):
        kv0 = 2 * conv_w.shape[2] + attn_o_w.shape[1]
        kv1 = kv0 + 2 * cache_k.shape[3] * cache_k.shape[4]
        reorder = lambda m: jnp.concatenate([m[..., :kv0], m[..., kv1:], m[..., kv0:kv1]], axis=-1)
        p = dict(norm1_g=norm1_g[l][None, :], w_in=reorder(w_in[l]).astype(BF16), b_in=reorder(b_in[l])[None, :],
                 conv_w=conv_w[l], conv_b=conv_b[l][None, :], ln_g=ln_g[l][None, :], ln_b=ln_b[l][None, :],
                 conv_out_w=conv_out_w[l].astype(BF16), q_norm_g=q_norm_g[l], k_norm_g=k_norm_g[l],
                 sinks=sinks[l], attn_o_w=attn_o_w[l].astype(BF16), w_out=w_out[l].astype(BF16),
                 norm2_g=norm2_g[l][None, :], ffn_up_w=ffn_up_w[l].astype(BF16), ffn_conv_w=ffn_conv_w[l],
                 ffn_conv_b=ffn_conv_b[l][None, :], ffn_down_w=ffn_down_w[l].astype(BF16))
        c_all = jnp.concatenate([c_prompt, c_sample], axis=0)
        mod3 = _mod(c_all, mod_w[l], mod_b[l][None, :])[:, None, :]
        zeros_conv = jnp.zeros((bp,) + cache_conv.shape[2:], F32)
        zeros_kv = jnp.zeros((bp,) + cache_k.shape[2:], F32)
        zeros_ffn = jnp.zeros((bp,) + cache_ffn_conv.shape[2:], F32)
        yp, nc_p, nk_p, nv_p, nf_p = _layer(yp, mod3, 0, pos_p, zeros_conv, zeros_kv, zeros_kv, zeros_ffn,
                                            True, p, tiles_p)
        ys, nc_s, nk_s, nv_s, nf_s = _layer(ys, mod3, bp, pos_s, cache_conv[l], cache_k[l], cache_v[l],
                                            cache_ffn_conv[l], False, p, tiles_s)
        for lst, val in zip(outs, (nc_p, nc_s, nk_p, nk_s, nv_p, nv_s, nf_p, nf_s)):
            lst.append(val)
    return (yp, ys) + tuple(jnp.stack(o) for o in outs)
```

```python
import functools

import jax
import jax.numpy as jnp
from jax import lax
from jax.experimental import pallas as pl
from jax.experimental.pallas import tpu as pltpu

CHUNK = 64
HEAD_DIM = 64
WINDOW = 128
PAST_LEN = 1024
ROPE_THETA = 10000.0
EPS = 1e-6
NEG_INF = -1e30
LANES = 128
SUBLANES = 8
VMEM_LIMIT_BYTES = 60 * 1024 * 1024

F32 = jnp.float32
BF16 = jnp.bfloat16


def _params(n_axes):
    return pltpu.CompilerParams(dimension_semantics=("arbitrary",) * n_axes,
                                vmem_limit_bytes=VMEM_LIMIT_BYTES)


def _sigmoid(x):
    return 1.0 / (1.0 + jnp.exp(-x))


def _mod_kernel(c_ref, w_ref, b_ref, o_ref):
    c = c_ref[...]
    a = (c * _sigmoid(c)).astype(BF16)
    o_ref[...] = jnp.dot(a, w_ref[...].astype(BF16), preferred_element_type=F32) + b_ref[...]


def _mod(c_all, mod_w, mod_b, tn=1024):
    m, d = c_all.shape
    n = mod_w.shape[1]
    return pl.pallas_call(
        _mod_kernel,
        grid=(n // tn,),
        in_specs=[pl.BlockSpec((m, d), lambda j: (0, 0)),
                  pl.BlockSpec((d, tn), lambda j: (0, j)),
                  pl.BlockSpec((1, tn), lambda j: (0, j))],
        out_specs=pl.BlockSpec((m, tn), lambda j: (0, j)),
        out_shape=jax.ShapeDtypeStruct((m, n), F32),
        compiler_params=_params(1),
        name="mod",
    )(c_all, mod_w, mod_b)


PROLOGUE_PARTS = 4


def _row_parts(nb, tt, parts):
    if nb >= parts:
        step = nb // parts
        return [(slice(i * step, (i + 1) * step), slice(0, tt)) for i in range(parts)]
    step = tt // parts
    return [(slice(b, b + 1), slice(i * step, (i + 1) * step)) for b in range(nb) for i in range(parts)]


def _adaln(x, g, sc, sh):
    ms = jnp.mean(x * x, axis=-1, keepdims=True)
    y = x * lax.rsqrt(ms + EPS) * g
    return y * (1.0 + sc) + sh


def _in_kernel(x_ref, sc_ref, sh_ref, g_ref, w_ref, b_ref, o_ref, h_ref):
    nb, tt, d = x_ref.shape
    j = pl.program_id(2)

    @pl.when(j == 0)
    def _():
        for bs, ts in _row_parts(nb, tt, PROLOGUE_PARTS):
            rows = (bs.stop - bs.start) * (ts.stop - ts.start)
            r0 = bs.start * tt + ts.start
            h = _adaln(x_ref[bs, ts, :], g_ref[...], sc_ref[bs], sh_ref[bs]).reshape(rows, d).astype(BF16)
            h_ref[r0:r0 + rows, :] = h
            z = jnp.dot(h, w_ref[...], preferred_element_type=F32) + b_ref[...]
            o_ref[bs, ts, :] = z.reshape(bs.stop - bs.start, ts.stop - ts.start, -1).astype(o_ref.dtype)

    @pl.when(j > 0)
    def _():
        z = jnp.dot(h_ref[...], w_ref[...], preferred_element_type=F32) + b_ref[...]
        o_ref[...] = z.reshape(nb, tt, -1).astype(o_ref.dtype)


def _in_proj(x, mod3, row0, norm_g, w, b, nb, tt, tn):
    bsz, t, d = x.shape
    n = w.shape[1]
    return pl.pallas_call(
        _in_kernel,
        grid=(bsz // nb, t // tt, n // tn),
        in_specs=[pl.BlockSpec((nb, tt, d), lambda i, s, j: (i, s, 0)),
                  pl.BlockSpec((nb, 1, d), lambda i, s, j: (row0 // nb + i, 0, 1)),
                  pl.BlockSpec((nb, 1, d), lambda i, s, j: (row0 // nb + i, 0, 0)),
                  pl.BlockSpec((1, d), lambda i, s, j: (0, 0)),
                  pl.BlockSpec((d, tn), lambda i, s, j: (0, j)),
                  pl.BlockSpec((1, tn), lambda i, s, j: (0, j))],
        out_specs=pl.BlockSpec((nb, tt, tn), lambda i, s, j: (i, s, j)),
        out_shape=jax.ShapeDtypeStruct((bsz, t, n), BF16),
        scratch_shapes=[pltpu.VMEM((nb * tt, d), BF16)],
        compiler_params=_params(3),
        name="in_proj",
    )(x, mod3, mod3, norm_g, w, b)


CONV_PAD = 32
CONV_STEPS = 16


def _slab_rows(rows):
    pitch = -(-rows // SUBLANES) * SUBLANES
    return pitch if (pitch // SUBLANES) % 2 else pitch + SUBLANES


def _conv_kernel(za_ref, zb_ref, ctx_ref, w_ref, b_ref, lg_ref, lb_ref, a_ref, nc_ref, gbuf, dwbuf):
    nb, tt, c = za_ref.shape
    nct = c // LANES
    kw = w_ref.shape[0]
    tp = gbuf.shape[1] // nct
    tp2 = dwbuf.shape[1] // nct
    lead = CONV_PAD - (kw - 1)
    t = pl.program_id(1)

    @pl.when(t == 0)
    def _():
        for j in range(nct):
            gbuf[:, j * tp:j * tp + CONV_PAD, :] = ctx_ref[:, :, j * LANES:(j + 1) * LANES]

    @pl.when(t > 0)
    def _():
        for j in range(nct):
            gbuf[:, j * tp:j * tp + CONV_PAD, :] = gbuf[:, j * tp + tt:j * tp + tt + CONV_PAD, :]

    for j in range(nct):
        ls = slice(j * LANES, (j + 1) * LANES)
        glu = za_ref[:, :, ls].astype(F32) * _sigmoid(zb_ref[:, :, ls].astype(F32))
        gbuf[:, j * tp + CONV_PAD:j * tp + CONV_PAD + tt, :] = glu
        nc_ref[:, :, ls] = gbuf[:, j * tp + tt:j * tp + tt + CONV_PAD, :]

    w = [w_ref[k] for k in range(kw)]
    bias = b_ref[...]
    nblk = tt // CONV_STEPS

    def body(i, carry):
        n = i // nblk
        t0 = (i % nblk) * CONV_STEPS
        acc = [bias] * CONV_STEPS
        for m in range(CONV_STEPS + kw - 1):
            g = gbuf[n, pl.ds(t0 + lead + m, SUBLANES, stride=tp), :]
            for s in range(max(0, m - (kw - 1)), min(CONV_STEPS - 1, m) + 1):
                acc[s] = acc[s] + g * w[m - s]
        for s in range(CONV_STEPS):
            dwbuf[n, pl.ds(t0 + s, SUBLANES, stride=tp2), :] = acc[s]
        return carry

    lax.fori_loop(0, nb * nblk, body, 0, unroll=2)

    slabs = [dwbuf[:, j * tp2:j * tp2 + tt, :] for j in range(nct)]
    mu = jnp.sum(functools.reduce(jnp.add, slabs), axis=-1, keepdims=True) * (1.0 / c)
    cen = [d - mu for d in slabs]
    var = jnp.sum(functools.reduce(jnp.add, [x * x for x in cen]), axis=-1, keepdims=True) * (1.0 / c)
    inv = lax.rsqrt(var + EPS)
    for j in range(nct):
        ls = slice(j * LANES, (j + 1) * LANES)
        y = cen[j] * inv * lg_ref[:, ls] + lb_ref[:, ls]
        a_ref[:, :, ls] = (y * _sigmoid(y)).astype(a_ref.dtype)


def _conv_branch(z, ctx_pad, conv_w, conv_b, ln_g, ln_b, nb, tt):
    bsz, t, _ = z.shape
    kw, c = conv_w.shape
    nct = c // LANES
    assert nct == SUBLANES and tt % CONV_STEPS == 0
    tp = _slab_rows(CONV_PAD + tt)
    tp2 = _slab_rows(tt)
    return pl.pallas_call(
        _conv_kernel,
        grid=(bsz // nb, t // tt),
        in_specs=[pl.BlockSpec((nb, tt, c), lambda i, s: (i, s, 0)),
                  pl.BlockSpec((nb, tt, c), lambda i, s: (i, s, 1)),
                  pl.BlockSpec((nb, CONV_PAD, c), lambda i, s: (i, 0, 0)),
                  pl.BlockSpec((kw, nct, LANES), lambda i, s: (0, 0, 0)),
                  pl.BlockSpec((nct, LANES), lambda i, s: (0, 0)),
                  pl.BlockSpec((1, c), lambda i, s: (0, 0)),
                  pl.BlockSpec((1, c), lambda i, s: (0, 0))],
        out_specs=[pl.BlockSpec((nb, tt, c), lambda i, s: (i, s, 0)),
                   pl.BlockSpec((nb, CONV_PAD, c), lambda i, s: (i, 0, 0))],
        out_shape=[jax.ShapeDtypeStruct((bsz, t, c), BF16),
                   jax.ShapeDtypeStruct((bsz, CONV_PAD, c), F32)],
        scratch_shapes=[pltpu.VMEM((nb, nct * tp, LANES), F32),
                        pltpu.VMEM((nb, nct * tp2, LANES), F32)],
        compiler_params=_params(2),
        name="conv_branch",
    )(z, z, ctx_pad, conv_w.reshape(kw, nct, LANES), conv_b.reshape(nct, LANES), ln_g, ln_b)


HEAD_BLOCK = 256
ATTN_UNROLL = 8


def _split_dot(x, w, split):
    hi = x.astype(BF16)
    out = jnp.dot(hi, w, preferred_element_type=F32)
    if split:
        lo = (x - hi.astype(F32)).astype(BF16)
        out = out + jnp.dot(lo, w, preferred_element_type=F32)
    return out


def _heads_norm_rope(x, g, cos, sin, mean_w, swap_w, split):
    ms = _split_dot(x * x, mean_w, split)
    y = x * lax.rsqrt(ms + EPS) * g
    return y * cos + _split_dot(y, swap_w, split) * sin


def _attn_kernel(sink_ref, q_ref, k_ref, v_ref, kc_ref, vc_ref, cos_ref, sin_ref, qg_ref, kg_ref, mw_ref, sw_ref,
                 o_ref, nk_ref, nv_ref, kf, vf, kx, vx, qs, s_scr, p_scr, *, mask_prefix):
    tq = q_ref.shape[1]
    kvw = k_ref.shape[2]
    n_kv = kvw // HEAD_DIM
    t = pl.program_id(1)
    reps = HEAD_BLOCK // LANES
    cos = jnp.concatenate([cos_ref[...]] * reps, axis=1)
    sin = jnp.concatenate([sin_ref[...]] * reps, axis=1)

    @pl.when(t == 0)
    def _():
        kf[0:WINDOW, :] = kc_ref[0]
        vf[0:WINDOW, :] = vc_ref[0]

    @pl.when(t > 0)
    def _():
        kf[0:WINDOW, :] = kf[tq:tq + WINDOW, :]
        vf[0:WINDOW, :] = vf[tq:tq + WINDOW, :]

    for j in range(kvw // HEAD_BLOCK):
        ls = slice(j * HEAD_BLOCK, (j + 1) * HEAD_BLOCK)
        kf[WINDOW:WINDOW + tq, ls] = _heads_norm_rope(k_ref[0, :, ls].astype(F32), kg_ref[...], cos, sin,
                                                     mw_ref[...], sw_ref[...], True)
    vf[WINDOW:WINDOW + tq, :] = v_ref[0].astype(F32)
    nk_ref[0] = kf[tq:tq + WINDOW, :]
    nv_ref[0] = vf[tq:tq + WINDOW, :]

    n_buf = WINDOW + tq
    lane = lax.broadcasted_iota(jnp.int32, (n_buf, LANES), 1)
    first = lane < HEAD_DIM
    for src, dst, fill in ((kf, kx, 0.0), (vf, vx, 1.0)):
        for j in range(kvw // LANES):
            tile = src[:, j * LANES:(j + 1) * LANES]
            swapped = pltpu.roll(tile, HEAD_DIM, 1)
            variants = (jnp.where(first, tile, fill), jnp.where(first, fill, swapped),
                        jnp.where(first, swapped, fill), jnp.where(first, fill, tile))
            for m, val in enumerate(variants):
                dst[0:n_buf, (4 * j + m) * LANES:(4 * j + m + 1) * LANES] = val.astype(BF16)
    kx[n_buf:n_buf + CHUNK, :] = jnp.zeros((CHUNK, kx.shape[1]), BF16)

    log2e = 1.4426950408889634
    scale = HEAD_DIM ** -0.5 * log2e
    for j in range(q_ref.shape[2] // HEAD_BLOCK):
        ls = slice(j * HEAD_BLOCK, (j + 1) * HEAD_BLOCK)
        qn = _heads_norm_rope(q_ref[0, :, ls].astype(F32), qg_ref[...], cos, sin,
                              mw_ref[...], sw_ref[...], False)
        qs[:, ls] = (qn * scale).astype(BF16)

    n_keys = WINDOW + CHUNK
    n_cols = n_keys + CHUNK
    n_chunks = tq // CHUNK
    rows = 2 * CHUNK
    dn = (((1,), (1,)), ((), ()))
    lane2 = lax.broadcasted_iota(jnp.int32, (rows, LANES), 1)
    row2 = lax.broadcasted_iota(jnp.int32, (rows, LANES), 0)
    first2 = lane2 < HEAD_DIM

    def score_body(c, carry):
        r0 = pl.multiple_of(c * CHUNK, CHUNK)
        for h in range(n_kv):
            q2 = jnp.concatenate([qs[pl.ds(r0, CHUNK), (2 * h) * LANES:(2 * h + 1) * LANES],
                                  qs[pl.ds(r0, CHUNK), (2 * h + 1) * LANES:(2 * h + 2) * LANES]], axis=0)
            for v in range(2):
                kt = kx[pl.ds(r0, n_cols), (2 * h + v) * LANES:(2 * h + v + 1) * LANES]
                s = lax.dot_general(q2, kt, dn, preferred_element_type=F32)
                sink = jnp.where(row2 < CHUNK, sink_ref[4 * h + v], sink_ref[4 * h + 2 + v]) * log2e
                pad = jnp.where(lane2 == n_keys - LANES, sink, NEG_INF)
                blk = 2 * (c * n_kv + h) + v
                s_scr[blk, :, 0:LANES] = s[:, 0:LANES]
                s_scr[blk, :, LANES:n_cols] = jnp.where(first2, s[:, LANES:n_cols], pad)
        return carry

    lax.fori_loop(0, n_chunks, score_body, 0, unroll=min(ATTN_UNROLL, n_chunks))

    if mask_prefix:
        for c in range(min(2, n_chunks)):
            g = t * n_chunks + c
            blocks = slice(2 * c * n_kv, 2 * (c + 1) * n_kv)

            @pl.when(g == 0)
            def _():
                s_scr[blocks, :, 0:LANES] = jnp.full((2 * n_kv, rows, LANES), NEG_INF, F32)

            @pl.when(g == 1)
            def _():
                s_scr[blocks, :, 0:LANES] = jnp.where(first2[None], NEG_INF, s_scr[blocks, :, 0:LANES])

    s_all = s_scr[...]
    m_all = jnp.max(s_all, axis=-1, keepdims=True)
    p_scr[...] = jnp.exp2(s_all - m_all).astype(BF16)

    e_row = lax.broadcasted_iota(jnp.int32, (CHUNK, LANES), 0) == 0
    e_lane = lax.broadcasted_iota(jnp.int32, (CHUNK, LANES), 1) < HEAD_DIM
    e_lo = jnp.where(jnp.logical_and(e_row, jnp.logical_not(e_lane)), 1.0, 0.0).astype(BF16)
    e_hi = jnp.where(jnp.logical_and(e_row, e_lane), 1.0, 0.0).astype(BF16)

    def out_body(c, carry):
        r0 = pl.multiple_of(c * CHUNK, CHUNK)
        for h in range(n_kv):
            blk = 2 * (c * n_kv + h)
            v_lo = jnp.concatenate([vx[pl.ds(r0, n_keys), (2 * h) * LANES:(2 * h + 1) * LANES], e_lo], axis=0)
            v_hi = jnp.concatenate([vx[pl.ds(r0, n_keys), (2 * h + 1) * LANES:(2 * h + 2) * LANES], e_hi], axis=0)
            o_lo = jnp.dot(p_scr[blk], v_lo, preferred_element_type=F32)
            o_hi = jnp.dot(p_scr[blk + 1], v_hi, preferred_element_type=F32)
            num = jnp.where(first2, o_lo, o_hi)
            den = pltpu.roll(jnp.where(first2, o_hi, o_lo), HEAD_DIM, 1)
            o2 = (num / den).astype(o_ref.dtype)
            o_ref[0, pl.ds(r0, CHUNK), (2 * h) * LANES:(2 * h + 1) * LANES] = o2[0:CHUNK]
            o_ref[0, pl.ds(r0, CHUNK), (2 * h + 1) * LANES:(2 * h + 2) * LANES] = o2[CHUNK:2 * CHUNK]
        return carry

    lax.fori_loop(0, n_chunks, out_body, 0, unroll=min(ATTN_UNROLL, n_chunks))


def _attention(z, k_cache, v_cache, cos, sin, q_g, k_g, sinks, q_col, k_col, aw, kvw, tq, mask_prefix):
    bsz, t, _ = z.shape
    assert q_col % aw == 0 and k_col % kvw == 0 and aw // kvw == 4 and kvw % LANES == 0
    k_blk = k_col // kvw
    n_blocks = 2 * (tq // CHUNK) * (kvw // HEAD_DIM)
    assert kvw % HEAD_BLOCK == 0 and aw % HEAD_BLOCK == 0
    idx = jnp.arange(HEAD_BLOCK)
    mean_w = jnp.where(idx[:, None] // HEAD_DIM == idx[None, :] // HEAD_DIM, 1.0 / HEAD_DIM, 0.0).astype(BF16)
    swap_w = (idx[:, None] == (idx[None, :] ^ (HEAD_DIM // 2))).astype(BF16)
    return pl.pallas_call(
        functools.partial(_attn_kernel, mask_prefix=mask_prefix),
        grid=(bsz, t // tq),
        in_specs=[pl.BlockSpec(memory_space=pltpu.SMEM),
                  pl.BlockSpec((1, tq, aw), lambda i, s: (i, s, q_col // aw)),
                  pl.BlockSpec((1, tq, kvw), lambda i, s: (i, s, k_blk)),
                  pl.BlockSpec((1, tq, kvw), lambda i, s: (i, s, k_blk + 1)),
                  pl.BlockSpec((1, WINDOW, kvw), lambda i, s: (i, 0, 0)),
                  pl.BlockSpec((1, WINDOW, kvw), lambda i, s: (i, 0, 0)),
                  pl.BlockSpec((tq, LANES), lambda i, s: (s, 0)),
                  pl.BlockSpec((tq, LANES), lambda i, s: (s, 0)),
                  pl.BlockSpec((1, HEAD_BLOCK), lambda i, s: (0, 0)),
                  pl.BlockSpec((1, HEAD_BLOCK), lambda i, s: (0, 0)),
                  pl.BlockSpec((HEAD_BLOCK, HEAD_BLOCK), lambda i, s: (0, 0)),
                  pl.BlockSpec((HEAD_BLOCK, HEAD_BLOCK), lambda i, s: (0, 0))],
        out_specs=[pl.BlockSpec((1, tq, aw), lambda i, s: (i, s, 0)),
                   pl.BlockSpec((1, WINDOW, kvw), lambda i, s: (i, 0, 0)),
                   pl.BlockSpec((1, WINDOW, kvw), lambda i, s: (i, 0, 0))],
        scratch_shapes=[pltpu.VMEM((WINDOW + tq, kvw), F32),
                        pltpu.VMEM((WINDOW + tq, kvw), F32),
                        pltpu.VMEM((WINDOW + tq + CHUNK, 4 * kvw), BF16),
                        pltpu.VMEM((WINDOW + tq, 4 * kvw), BF16),
                        pltpu.VMEM((tq, aw), BF16),
                        pltpu.VMEM((n_blocks, 2 * CHUNK, WINDOW + 2 * CHUNK), F32),
                        pltpu.VMEM((n_blocks, 2 * CHUNK, WINDOW + 2 * CHUNK), BF16)],
        out_shape=[jax.ShapeDtypeStruct((bsz, t, aw), BF16),
                   jax.ShapeDtypeStruct((bsz, WINDOW, kvw), F32),
                   jax.ShapeDtypeStruct((bsz, WINDOW, kvw), F32)],
        compiler_params=_params(2),
        name="attention",
    )(sinks, z, z, z, k_cache, v_cache, cos, sin, q_g, k_g, mean_w, swap_w)


MERGE_SUB = 256


def _merge_kernel(a_ref, o_ref, gc_ref, ga_ref, wc_ref, wa_ref, wo_ref, x_ref, g1_ref, y_ref, mg):
    nb, tt, d = x_ref.shape
    tn = gc_ref.shape[2]
    n = pl.program_id(2)
    a = a_ref[...].reshape(nb * tt, -1)
    o = o_ref[...].reshape(nb * tt, -1)
    for c0 in range(0, tn, MERGE_SUB):
        cs = slice(c0, c0 + MERGE_SUB)
        ws = pl.ds(pl.multiple_of(n * tn + c0, MERGE_SUB), MERGE_SUB)
        yc = jnp.dot(a, wc_ref[:, ws], preferred_element_type=F32)
        ya = jnp.dot(o, wa_ref[:, ws], preferred_element_type=F32)
        gc = gc_ref[:, :, cs].astype(F32).reshape(nb * tt, MERGE_SUB)
        ga = ga_ref[:, :, cs].astype(F32).reshape(nb * tt, MERGE_SUB)
        merged = _sigmoid(gc) * yc + _sigmoid(ga) * ya
        mg[:, ws] = merged.astype(BF16)

    @pl.when(n == pl.num_programs(2) - 1)
    def _():
        proj = jnp.dot(mg[...], wo_ref[...], preferred_element_type=F32).reshape(nb, tt, d)
        y_ref[...] = x_ref[...] + g1_ref[...] * proj


def _merge(a, o, z, gc_col, ga_col, wc, wa, wo, x, mod3, row0, nb, tt, tn):
    bsz, t, d = x.shape
    c = a.shape[2]
    aw = o.shape[2]
    assert gc_col % tn == 0 and ga_col % tn == 0 and d % tn == 0
    return pl.pallas_call(
        _merge_kernel,
        grid=(bsz // nb, t // tt, d // tn),
        in_specs=[pl.BlockSpec((nb, tt, c), lambda i, s, n: (i, s, 0)),
                  pl.BlockSpec((nb, tt, aw), lambda i, s, n: (i, s, 0)),
                  pl.BlockSpec((nb, tt, tn), lambda i, s, n: (i, s, gc_col // tn + n)),
                  pl.BlockSpec((nb, tt, tn), lambda i, s, n: (i, s, ga_col // tn + n)),
                  pl.BlockSpec((c, d), lambda i, s, n: (0, 0), pipeline_mode=pl.Buffered(1)),
                  pl.BlockSpec((aw, d), lambda i, s, n: (0, 0), pipeline_mode=pl.Buffered(1)),
                  pl.BlockSpec((d, d), lambda i, s, n: (0, 0), pipeline_mode=pl.Buffered(1)),
                  pl.BlockSpec((nb, tt, d), lambda i, s, n: (i, s, 0)),
                  pl.BlockSpec((nb, 1, d), lambda i, s, n: (row0 // nb + i, 0, 2))],
        out_specs=pl.BlockSpec((nb, tt, d), lambda i, s, n: (i, s, 0)),
        out_shape=jax.ShapeDtypeStruct((bsz, t, d), F32),
        scratch_shapes=[pltpu.VMEM((nb * tt, d), BF16)],
        compiler_params=_params(3),
        name="merge_out",
    )(a, o, z, z, wc, wa, wo, x, mod3)


FFN_PAD = SUBLANES
FFN_PARTS = 2


def _ffn_kernel(x_ref, sc_ref, sh_ref, g2_ref, ng_ref, wg_ref, wv_ref, cw_ref, cb_ref, wd_ref, ctx_ref,
                y_ref, nf_ref, h_ref, ubuf, halo):
    nb, tt, d = x_ref.shape
    tf = wg_ref.shape[1]
    kw = cw_ref.shape[0]
    t = pl.program_id(1)
    f = pl.program_id(2)

    @pl.when(t == 0)
    def _():
        ubuf[:, 0:FFN_PAD, :] = ctx_ref[...]

    @pl.when(t > 0)
    def _():
        ubuf[:, 0:FFN_PAD, :] = halo[f]

    def step(first):
        cbias = cb_ref[...].reshape(1, 1, tf)
        taps = [cw_ref[k:k + 1, :].reshape(1, 1, tf) for k in range(kw)]
        for bs, ts in _row_parts(nb, tt, FFN_PARTS):
            nbp, ttp = bs.stop - bs.start, ts.stop - ts.start
            r0 = bs.start * tt + ts.start
            if first:
                hp = _adaln(x_ref[bs, ts, :], ng_ref[...], sc_ref[bs], sh_ref[bs])
                hp = hp.reshape(nbp * ttp, d).astype(BF16)
                h_ref[r0:r0 + nbp * ttp, :] = hp
            else:
                hp = h_ref[r0:r0 + nbp * ttp, :]
            ug = jnp.dot(hp, wg_ref[...], preferred_element_type=F32)
            uv = jnp.dot(hp, wv_ref[...], preferred_element_type=F32)
            ubuf[bs, FFN_PAD + ts.start:FFN_PAD + ts.stop, :] = ug.reshape(nbp, ttp, tf)
            conv = jnp.broadcast_to(cbias, (nbp, ttp, tf))
            for k in range(kw):
                lead = FFN_PAD - (kw - 1) + k + ts.start
                conv = conv + ubuf[bs, lead:lead + ttp, :] * taps[k]
            act = (conv * _sigmoid(conv)).reshape(nbp * ttp, tf) * uv
            part = jnp.dot(act.astype(BF16), wd_ref[...], preferred_element_type=F32).reshape(nbp, ttp, d)
            if first:
                y_ref[bs, ts, :] = part
            else:
                y_ref[bs, ts, :] += part
        tail = ubuf[:, tt:tt + FFN_PAD, :]
        halo[f] = tail
        nf_ref[:, :, pl.ds(pl.multiple_of(f * tf, tf), tf)] = tail

    pl.when(f == 0)(functools.partial(step, True))
    pl.when(f > 0)(functools.partial(step, False))

    @pl.when(f == pl.num_programs(2) - 1)
    def _():
        y_ref[...] = x_ref[...] + g2_ref[...] * y_ref[...]


def _ffn(x, mod3, row0, norm_g, w_up, conv_w, conv_b, w_down, ctx_pad, nb, tt, tf):
    bsz, t, d = x.shape
    dff = w_down.shape[0]
    kw = conv_w.shape[0]
    nf = dff // tf
    assert dff % tf == 0
    return pl.pallas_call(
        _ffn_kernel,
        grid=(bsz // nb, t // tt, nf),
        in_specs=[pl.BlockSpec((nb, tt, d), lambda i, s, f: (i, s, 0)),
                  pl.BlockSpec((nb, 1, d), lambda i, s, f: (row0 // nb + i, 0, 4)),
                  pl.BlockSpec((nb, 1, d), lambda i, s, f: (row0 // nb + i, 0, 3)),
                  pl.BlockSpec((nb, 1, d), lambda i, s, f: (row0 // nb + i, 0, 5)),
                  pl.BlockSpec((1, d), lambda i, s, f: (0, 0)),
                  pl.BlockSpec((d, tf), lambda i, s, f: (0, f)),
                  pl.BlockSpec((d, tf), lambda i, s, f: (0, nf + f)),
                  pl.BlockSpec((kw, tf), lambda i, s, f: (0, f)),
                  pl.BlockSpec((1, tf), lambda i, s, f: (0, f)),
                  pl.BlockSpec((tf, d), lambda i, s, f: (f, 0)),
                  pl.BlockSpec((nb, FFN_PAD, tf), lambda i, s, f: (i, 0, f))],
        out_specs=[pl.BlockSpec((nb, tt, d), lambda i, s, f: (i, s, 0)),
                   pl.BlockSpec((nb, FFN_PAD, dff), lambda i, s, f: (i, 0, 0))],
        out_shape=[jax.ShapeDtypeStruct((bsz, t, d), F32),
                   jax.ShapeDtypeStruct((bsz, FFN_PAD, dff), F32)],
        scratch_shapes=[pltpu.VMEM((nb * tt, d), BF16),
                        pltpu.VMEM((nb, FFN_PAD + tt, tf), F32),
                        pltpu.VMEM((nf, nb, FFN_PAD, tf), F32)],
        compiler_params=_params(3),
        name="conv_ffn",
    )(x, mod3, mod3, mod3, norm_g, w_up, w_up, conv_w, conv_b, w_down, ctx_pad)


def _rope_tables(pos):
    half = HEAD_DIM // 2
    inv_freq = 1.0 / (ROPE_THETA ** (jnp.arange(half, dtype=F32) / half))
    ang = pos.astype(F32)[:, None] * inv_freq[None, :]
    cos = jnp.cos(ang)
    sin = jnp.sin(ang)
    reps = LANES // HEAD_DIM
    return (jnp.tile(jnp.concatenate([cos, cos], axis=-1), (1, reps)),
            jnp.tile(jnp.concatenate([-sin, sin], axis=-1), (1, reps)))


def _front_pad(ctx, rows):
    return jnp.pad(ctx, ((0, 0), (rows - ctx.shape[1], 0), (0, 0)))


def _layer(x, mod3, row0, pos, conv_ctx, k_cache, v_cache, ffn_ctx, mask_prefix, p, tiles):
    bsz, t, d = x.shape
    c = p["conv_w"].shape[1]
    kvw = k_cache.shape[2] * k_cache.shape[3]
    aw = p["attn_o_w"].shape[0]
    nb, tt, tq = tiles["nb"], tiles["tt"], tiles["tq"]
    assert row0 % nb == 0 and bsz % nb == 0 and t % tt == 0 and t % tq == 0
    assert row0 % tiles["nb_merge"] == 0 and bsz % tiles["nb_merge"] == 0

    z = _in_proj(x, mod3, row0, p["norm1_g"], p["w_in"], p["b_in"], nb, tiles["tt_big"], tiles["tn_in"])
    a, nc = _conv_branch(z, _front_pad(conv_ctx, CONV_PAD), p["conv_w"], p["conv_b"], p["ln_g"], p["ln_b"],
                         nb, tiles["tt_big"])
    cos, sin = _rope_tables(pos)
    reps = HEAD_BLOCK // HEAD_DIM
    o, nk, nv = _attention(z, k_cache.reshape(bsz, -1, kvw), v_cache.reshape(bsz, -1, kvw), cos, sin,
                           jnp.tile(p["q_norm_g"], reps)[None, :], jnp.tile(p["k_norm_g"], reps)[None, :],
                           p["sinks"], 2 * c + 2 * d, 2 * c + 2 * d + aw, aw, kvw, tq, mask_prefix)
    gc_col = 2 * c
    x1 = _merge(a, o, z, gc_col, gc_col + d, p["conv_out_w"], p["attn_o_w"], p["w_out"], x, mod3, row0,
                tiles["nb_merge"], tt, tiles["tn_merge"])
    y, nf = _ffn(x1, mod3, row0, p["norm2_g"], p["ffn_up_w"], p["ffn_conv_w"], p["ffn_conv_b"],
                 p["ffn_down_w"], _front_pad(ffn_ctx, FFN_PAD), nb, tiles["tt_big"], tiles["tf"])
    kw = p["conv_w"].shape[0]
    fkw = p["ffn_conv_w"].shape[0]
    return (y, nc[:, CONV_PAD - (kw - 1):], nk.reshape(k_cache.shape), nv.reshape(v_cache.shape),
            nf[:, FFN_PAD - (fkw - 1):])


def kernel(x_prompt, x_sample, c_prompt, c_sample, cache_conv, cache_k, cache_v, cache_ffn_conv, mod_w, mod_b, norm1_g, w_in, b_in, conv_w, conv_b, ln_g, ln_b, conv_out_w, q_norm_g, k_norm_g, sinks, attn_o_w, w_out, norm2_g, ffn_up_w, ffn_conv_w, ffn_conv_b, ffn_down_w):
    depth = mod_w.shape[0]
    bp, tp, d = x_prompt.shape
    bs, ts, _ = x_sample.shape
    pos_p = jnp.arange(tp)
    pos_s = PAST_LEN + jnp.arange(ts)
    yp, ys = x_prompt, x_sample
    outs = [[] for _ in range(8)]
    tiles_p = dict(nb=1, nb_merge=1, tt=min(512, tp), tt_big=min(1024, tp), tq=min(1024, tp), tn_in=1536, tn_merge=2048,
                   tf=512)
    tiles_s = dict(nb=min(16, bs), nb_merge=min(8, bs), tt=ts, tt_big=ts, tq=ts, tn_in=1536, tn_merge=2048,
                   tf=512)
    for l in range(depth):
        glu_w = 2 * conv_w.shape[2]
        gates0 = glu_w + attn_o_w.shape[1] + 2 * cache_k.shape[3] * cache_k.shape[4]
        reorder = lambda m: jnp.concatenate([m[..., :glu_w], m[..., gates0:], m[..., glu_w:gates0]], axis=-1)
        p = dict(norm1_g=norm1_g[l][None, :], w_in=reorder(w_in[l]).astype(BF16), b_in=reorder(b_in[l])[None, :],
                 conv_w=conv_w[l], conv_b=conv_b[l][None, :], ln_g=ln_g[l][None, :], ln_b=ln_b[l][None, :],
                 conv_out_w=conv_out_w[l].astype(BF16), q_norm_g=q_norm_g[l], k_norm_g=k_norm_g[l],
                 sinks=sinks[l], attn_o_w=attn_o_w[l].astype(BF16), w_out=w_out[l].astype(BF16),
                 norm2_g=norm2_g[l][None, :], ffn_up_w=ffn_up_w[l].astype(BF16), ffn_conv_w=ffn_conv_w[l],
                 ffn_conv_b=ffn_conv_b[l][None, :], ffn_down_w=ffn_down_w[l].astype(BF16))
        c_all = jnp.concatenate([c_prompt, c_sample], axis=0)
        mod3 = _mod(c_all, mod_w[l], mod_b[l][None, :])[:, None, :]
        zeros_conv = jnp.zeros((bp,) + cache_conv.shape[2:], F32)
        zeros_kv = jnp.zeros((bp,) + cache_k.shape[2:], F32)
        zeros_ffn = jnp.zeros((bp,) + cache_ffn_conv.shape[2:], F32)
        yp, nc_p, nk_p, nv_p, nf_p = _layer(yp, mod3, 0, pos_p, zeros_conv, zeros_kv, zeros_kv, zeros_ffn,
                                            True, p, tiles_p)
        ys, nc_s, nk_s, nv_s, nf_s = _layer(ys, mod3, bp, pos_s, cache_conv[l], cache_k[l], cache_v[l],
                                            cache_ffn_conv[l], False, p, tiles_s)
        for lst, val in zip(outs, (nc_p, nc_s, nk_p, nk_s, nv_p, nv_s, nf_p, nf_s)):
            lst.append(val)
    return (yp, ys) + tuple(jnp.stack(o) for o in outs)
```

```python
import functools

import jax
import jax.numpy as jnp
from jax import lax
from jax.experimental import pallas as pl
from jax.experimental.pallas import tpu as pltpu

CHUNK = 64
HEAD_DIM = 64
WINDOW = 128
PAST_LEN = 1024
ROPE_THETA = 10000.0
EPS = 1e-6
NEG_INF = -1e30
LANES = 128
SUBLANES = 8
VMEM_LIMIT_BYTES = 60 * 1024 * 1024

F32 = jnp.float32
BF16 = jnp.bfloat16


def _params(n_axes):
    return pltpu.CompilerParams(dimension_semantics=("arbitrary",) * n_axes,
                                vmem_limit_bytes=VMEM_LIMIT_BYTES)


LOG2E = 1.4426950408889634


def _sigmoid(x):
    return 1.0 / (1.0 + jnp.exp2(x * (-LOG2E)))


def _mod_kernel(c_ref, w_ref, b_ref, o_ref):
    c = c_ref[...]
    a = (c * _sigmoid(c)).astype(BF16)
    o_ref[...] = jnp.dot(a, w_ref[...].astype(BF16), preferred_element_type=F32) + b_ref[...]


def _mod(c_all, mod_w, mod_b, tn=1024):
    m, d = c_all.shape
    n = mod_w.shape[1]
    return pl.pallas_call(
        _mod_kernel,
        grid=(n // tn,),
        in_specs=[pl.BlockSpec((m, d), lambda j: (0, 0)),
                  pl.BlockSpec((d, tn), lambda j: (0, j)),
                  pl.BlockSpec((1, tn), lambda j: (0, j))],
        out_specs=pl.BlockSpec((m, tn), lambda j: (0, j)),
        out_shape=jax.ShapeDtypeStruct((m, n), F32),
        compiler_params=_params(1),
        name="mod",
    )(c_all, mod_w, mod_b)


PROLOGUE_PARTS = 4


def _row_parts(nb, tt, parts):
    if nb >= parts:
        step = nb // parts
        return [(slice(i * step, (i + 1) * step), slice(0, tt)) for i in range(parts)]
    step = tt // parts
    return [(slice(b, b + 1), slice(i * step, (i + 1) * step)) for b in range(nb) for i in range(parts)]


def _adaln(x, g, sc, sh):
    ms = jnp.mean(x * x, axis=-1, keepdims=True)
    y = x * lax.rsqrt(ms + EPS) * g
    return y * (1.0 + sc) + sh


def _in_kernel(x_ref, sc_ref, sh_ref, g_ref, w_ref, b_ref, o_ref, h_ref):
    nb, tt, d = x_ref.shape
    j = pl.program_id(2)

    @pl.when(j == 0)
    def _():
        for bs, ts in _row_parts(nb, tt, PROLOGUE_PARTS):
            rows = (bs.stop - bs.start) * (ts.stop - ts.start)
            r0 = bs.start * tt + ts.start
            h = _adaln(x_ref[bs, ts, :], g_ref[...], sc_ref[bs], sh_ref[bs]).reshape(rows, d).astype(BF16)
            h_ref[r0:r0 + rows, :] = h
            z = jnp.dot(h, w_ref[...], preferred_element_type=F32) + b_ref[...]
            o_ref[bs, ts, :] = z.reshape(bs.stop - bs.start, ts.stop - ts.start, -1).astype(o_ref.dtype)

    @pl.when(j > 0)
    def _():
        z = jnp.dot(h_ref[...], w_ref[...], preferred_element_type=F32) + b_ref[...]
        o_ref[...] = z.reshape(nb, tt, -1).astype(o_ref.dtype)


def _in_proj(x, mod3, row0, norm_g, w, b, nb, tt, tn):
    bsz, t, d = x.shape
    n = w.shape[1]
    return pl.pallas_call(
        _in_kernel,
        grid=(bsz // nb, t // tt, n // tn),
        in_specs=[pl.BlockSpec((nb, tt, d), lambda i, s, j: (i, s, 0)),
                  pl.BlockSpec((nb, 1, d), lambda i, s, j: (row0 // nb + i, 0, 1)),
                  pl.BlockSpec((nb, 1, d), lambda i, s, j: (row0 // nb + i, 0, 0)),
                  pl.BlockSpec((1, d), lambda i, s, j: (0, 0)),
                  pl.BlockSpec((d, tn), lambda i, s, j: (0, j)),
                  pl.BlockSpec((1, tn), lambda i, s, j: (0, j))],
        out_specs=pl.BlockSpec((nb, tt, tn), lambda i, s, j: (i, s, j)),
        out_shape=jax.ShapeDtypeStruct((bsz, t, n), BF16),
        scratch_shapes=[pltpu.VMEM((nb * tt, d), BF16)],
        compiler_params=_params(3),
        name="in_proj",
    )(x, mod3, mod3, norm_g, w, b)


CONV_PAD = 32
CONV_STEPS = 16


def _slab_rows(rows):
    pitch = -(-rows // SUBLANES) * SUBLANES
    return pitch if (pitch // SUBLANES) % 2 else pitch + SUBLANES


def _conv_kernel(za_ref, zb_ref, ctx_ref, w_ref, b_ref, lg_ref, lb_ref, a_ref, nc_ref, gbuf, dwbuf):
    nb, tt, c = za_ref.shape
    nct = c // LANES
    kw = w_ref.shape[0]
    tp = gbuf.shape[1] // nct
    tp2 = dwbuf.shape[1] // nct
    lead = CONV_PAD - (kw - 1)
    t = pl.program_id(1)

    @pl.when(t == 0)
    def _():
        for j in range(nct):
            gbuf[:, j * tp:j * tp + CONV_PAD, :] = ctx_ref[:, :, j * LANES:(j + 1) * LANES]

    @pl.when(t > 0)
    def _():
        for j in range(nct):
            gbuf[:, j * tp:j * tp + CONV_PAD, :] = gbuf[:, j * tp + tt:j * tp + tt + CONV_PAD, :]

    for j in range(nct):
        ls = slice(j * LANES, (j + 1) * LANES)
        glu = za_ref[:, :, ls].astype(F32) * _sigmoid(zb_ref[:, :, ls].astype(F32))
        gbuf[:, j * tp + CONV_PAD:j * tp + CONV_PAD + tt, :] = glu
        nc_ref[:, :, ls] = gbuf[:, j * tp + tt:j * tp + tt + CONV_PAD, :]

    w = [w_ref[k] for k in range(kw)]
    bias = b_ref[...]
    nblk = tt // CONV_STEPS

    def body(i, carry):
        n = i // nblk
        t0 = (i % nblk) * CONV_STEPS
        acc = [bias] * CONV_STEPS
        for m in range(CONV_STEPS + kw - 1):
            g = gbuf[n, pl.ds(t0 + lead + m, SUBLANES, stride=tp), :]
            for s in range(max(0, m - (kw - 1)), min(CONV_STEPS - 1, m) + 1):
                acc[s] = acc[s] + g * w[m - s]
        for s in range(CONV_STEPS):
            dwbuf[n, pl.ds(t0 + s, SUBLANES, stride=tp2), :] = acc[s]
        return carry

    lax.fori_loop(0, nb * nblk, body, 0, unroll=4)

    slabs = [dwbuf[:, j * tp2:j * tp2 + tt, :] for j in range(nct)]
    mu = jnp.sum(functools.reduce(jnp.add, slabs), axis=-1, keepdims=True) * (1.0 / c)
    cen = [d - mu for d in slabs]
    var = jnp.sum(functools.reduce(jnp.add, [x * x for x in cen]), axis=-1, keepdims=True) * (1.0 / c)
    inv = lax.rsqrt(var + EPS)
    for j in range(nct):
        ls = slice(j * LANES, (j + 1) * LANES)
        y = cen[j] * inv * lg_ref[:, ls] + lb_ref[:, ls]
        a_ref[:, :, ls] = (y * _sigmoid(y)).astype(a_ref.dtype)


def _conv_branch(z, ctx_pad, conv_w, conv_b, ln_g, ln_b, nb, tt):
    bsz, t, _ = z.shape
    kw, c = conv_w.shape
    nct = c // LANES
    assert nct == SUBLANES and tt % CONV_STEPS == 0
    tp = _slab_rows(CONV_PAD + tt)
    tp2 = _slab_rows(tt)
    return pl.pallas_call(
        _conv_kernel,
        grid=(bsz // nb, t // tt),
        in_specs=[pl.BlockSpec((nb, tt, c), lambda i, s: (i, s, 0)),
                  pl.BlockSpec((nb, tt, c), lambda i, s: (i, s, 1)),
                  pl.BlockSpec((nb, CONV_PAD, c), lambda i, s: (i, 0, 0)),
                  pl.BlockSpec((kw, nct, LANES), lambda i, s: (0, 0, 0)),
                  pl.BlockSpec((nct, LANES), lambda i, s: (0, 0)),
                  pl.BlockSpec((1, c), lambda i, s: (0, 0)),
                  pl.BlockSpec((1, c), lambda i, s: (0, 0))],
        out_specs=[pl.BlockSpec((nb, tt, c), lambda i, s: (i, s, 0)),
                   pl.BlockSpec((nb, CONV_PAD, c), lambda i, s: (i, 0, 0))],
        out_shape=[jax.ShapeDtypeStruct((bsz, t, c), BF16),
                   jax.ShapeDtypeStruct((bsz, CONV_PAD, c), F32)],
        scratch_shapes=[pltpu.VMEM((nb, nct * tp, LANES), F32),
                        pltpu.VMEM((nb, nct * tp2, LANES), F32)],
        compiler_params=_params(2),
        name="conv_branch",
    )(z, z, ctx_pad, conv_w.reshape(kw, nct, LANES), conv_b.reshape(nct, LANES), ln_g, ln_b)


HEAD_BLOCK = 256
ATTN_UNROLL = 8


def _split_dot(x, w, split):
    hi = x.astype(BF16)
    out = jnp.dot(hi, w, preferred_element_type=F32)
    if split:
        lo = (x - hi.astype(F32)).astype(BF16)
        out = out + jnp.dot(lo, w, preferred_element_type=F32)
    return out


def _heads_norm_rope(x, g, cos, sin, mean_w, swap_w, split):
    ms = _split_dot(x * x, mean_w, split)
    y = x * lax.rsqrt(ms + EPS) * g
    return y * cos + _split_dot(y, swap_w, split) * sin


def _attn_kernel(sink_ref, q_ref, k_ref, v_ref, kc_ref, vc_ref, cos_ref, sin_ref, qg_ref, kg_ref, mw_ref, sw_ref,
                 o_ref, nk_ref, nv_ref, kf, vf, kx, vx, qs, s_scr, p_scr, *, mask_prefix):
    tq = q_ref.shape[1]
    kvw = k_ref.shape[2]
    n_kv = kvw // HEAD_DIM
    t = pl.program_id(1)
    reps = HEAD_BLOCK // LANES
    cos = jnp.concatenate([cos_ref[...]] * reps, axis=1)
    sin = jnp.concatenate([sin_ref[...]] * reps, axis=1)

    @pl.when(t == 0)
    def _():
        kf[0:WINDOW, :] = kc_ref[0]
        vf[0:WINDOW, :] = vc_ref[0]

    @pl.when(t > 0)
    def _():
        kf[0:WINDOW, :] = kf[tq:tq + WINDOW, :]
        vf[0:WINDOW, :] = vf[tq:tq + WINDOW, :]

    for j in range(kvw // HEAD_BLOCK):
        ls = slice(j * HEAD_BLOCK, (j + 1) * HEAD_BLOCK)
        kf[WINDOW:WINDOW + tq, ls] = _heads_norm_rope(k_ref[0, :, ls].astype(F32), kg_ref[...], cos, sin,
                                                     mw_ref[...], sw_ref[...], True)
    vf[WINDOW:WINDOW + tq, :] = v_ref[0].astype(F32)
    nk_ref[0] = kf[tq:tq + WINDOW, :]
    nv_ref[0] = vf[tq:tq + WINDOW, :]

    n_buf = WINDOW + tq
    lane = lax.broadcasted_iota(jnp.int32, (n_buf, LANES), 1)
    first = lane < HEAD_DIM
    for src, dst, fill in ((kf, kx, 0.0), (vf, vx, 1.0)):
        for j in range(kvw // LANES):
            tile = src[:, j * LANES:(j + 1) * LANES]
            swapped = pltpu.roll(tile, HEAD_DIM, 1)
            variants = (jnp.where(first, tile, fill), jnp.where(first, fill, swapped),
                        jnp.where(first, swapped, fill), jnp.where(first, fill, tile))
            for m, val in enumerate(variants):
                dst[0:n_buf, (4 * j + m) * LANES:(4 * j + m + 1) * LANES] = val.astype(BF16)
    kx[n_buf:n_buf + CHUNK, :] = jnp.zeros((CHUNK, kx.shape[1]), BF16)

    scale = HEAD_DIM ** -0.5 * LOG2E
    cos_q = cos * scale
    sin_q = sin * scale
    for j in range(q_ref.shape[2] // HEAD_BLOCK):
        ls = slice(j * HEAD_BLOCK, (j + 1) * HEAD_BLOCK)
        qn = _heads_norm_rope(q_ref[0, :, ls].astype(F32), qg_ref[...], cos_q, sin_q,
                              mw_ref[...], sw_ref[...], False)
        qs[:, ls] = qn.astype(BF16)

    n_keys = WINDOW + CHUNK
    n_cols = n_keys + CHUNK
    n_chunks = tq // CHUNK
    rows = 2 * CHUNK
    dn = (((1,), (1,)), ((), ()))
    lane2 = lax.broadcasted_iota(jnp.int32, (rows, LANES), 1)
    row2 = lax.broadcasted_iota(jnp.int32, (rows, LANES), 0)
    first2 = lane2 < HEAD_DIM

    def score_body(c, carry):
        r0 = pl.multiple_of(c * CHUNK, CHUNK)
        for h in range(n_kv):
            q2 = jnp.concatenate([qs[pl.ds(r0, CHUNK), (2 * h) * LANES:(2 * h + 1) * LANES],
                                  qs[pl.ds(r0, CHUNK), (2 * h + 1) * LANES:(2 * h + 2) * LANES]], axis=0)
            for v in range(2):
                kt = kx[pl.ds(r0, n_cols), (2 * h + v) * LANES:(2 * h + v + 1) * LANES]
                s = lax.dot_general(q2, kt, dn, preferred_element_type=F32)
                sink = jnp.where(row2 < CHUNK, sink_ref[4 * h + v], sink_ref[4 * h + 2 + v]) * LOG2E
                pad = jnp.where(lane2 == n_keys - LANES, sink, NEG_INF)
                blk = 2 * (c * n_kv + h) + v
                s_scr[blk, :, 0:LANES] = s[:, 0:LANES]
                s_scr[blk, :, LANES:n_cols] = jnp.where(first2, s[:, LANES:n_cols], pad)
        return carry

    lax.fori_loop(0, n_chunks, score_body, 0, unroll=min(ATTN_UNROLL, n_chunks))

    if mask_prefix:
        for c in range(min(2, n_chunks)):
            g = t * n_chunks + c
            blocks = slice(2 * c * n_kv, 2 * (c + 1) * n_kv)

            @pl.when(g == 0)
            def _():
                s_scr[blocks, :, 0:LANES] = jnp.full((2 * n_kv, rows, LANES), NEG_INF, F32)

            @pl.when(g == 1)
            def _():
                s_scr[blocks, :, 0:LANES] = jnp.where(first2[None], NEG_INF, s_scr[blocks, :, 0:LANES])

    s_all = s_scr[...]
    m_all = jnp.max(s_all, axis=-1, keepdims=True)
    p_scr[...] = jnp.exp2(s_all - m_all).astype(BF16)

    e_row = lax.broadcasted_iota(jnp.int32, (CHUNK, LANES), 0) == 0
    e_lane = lax.broadcasted_iota(jnp.int32, (CHUNK, LANES), 1) < HEAD_DIM
    e_lo = jnp.where(jnp.logical_and(e_row, jnp.logical_not(e_lane)), 1.0, 0.0).astype(BF16)
    e_hi = jnp.where(jnp.logical_and(e_row, e_lane), 1.0, 0.0).astype(BF16)

    def out_body(c, carry):
        r0 = pl.multiple_of(c * CHUNK, CHUNK)
        for h in range(n_kv):
            blk = 2 * (c * n_kv + h)
            v_lo = jnp.concatenate([vx[pl.ds(r0, n_keys), (2 * h) * LANES:(2 * h + 1) * LANES], e_lo], axis=0)
            v_hi = jnp.concatenate([vx[pl.ds(r0, n_keys), (2 * h + 1) * LANES:(2 * h + 2) * LANES], e_hi], axis=0)
            o_lo = jnp.dot(p_scr[blk], v_lo, preferred_element_type=F32)
            o_hi = jnp.dot(p_scr[blk + 1], v_hi, preferred_element_type=F32)
            num = jnp.where(first2, o_lo, o_hi)
            den = pltpu.roll(jnp.where(first2, o_hi, o_lo), HEAD_DIM, 1)
            o2 = (num / den).astype(o_ref.dtype)
            o_ref[0, pl.ds(r0, CHUNK), (2 * h) * LANES:(2 * h + 1) * LANES] = o2[0:CHUNK]
            o_ref[0, pl.ds(r0, CHUNK), (2 * h + 1) * LANES:(2 * h + 2) * LANES] = o2[CHUNK:2 * CHUNK]
        return carry

    lax.fori_loop(0, n_chunks, out_body, 0, unroll=min(ATTN_UNROLL, n_chunks))


def _attention(z, k_cache, v_cache, cos, sin, q_g, k_g, sinks, q_col, k_col, aw, kvw, tq, mask_prefix):
    bsz, t, _ = z.shape
    assert q_col % aw == 0 and k_col % kvw == 0 and aw // kvw == 4 and kvw % LANES == 0
    k_blk = k_col // kvw
    n_blocks = 2 * (tq // CHUNK) * (kvw // HEAD_DIM)
    assert kvw % HEAD_BLOCK == 0 and aw % HEAD_BLOCK == 0
    idx = jnp.arange(HEAD_BLOCK)
    mean_w = jnp.where(idx[:, None] // HEAD_DIM == idx[None, :] // HEAD_DIM, 1.0 / HEAD_DIM, 0.0).astype(BF16)
    swap_w = (idx[:, None] == (idx[None, :] ^ (HEAD_DIM // 2))).astype(BF16)
    return pl.pallas_call(
        functools.partial(_attn_kernel, mask_prefix=mask_prefix),
        grid=(bsz, t // tq),
        in_specs=[pl.BlockSpec(memory_space=pltpu.SMEM),
                  pl.BlockSpec((1, tq, aw), lambda i, s: (i, s, q_col // aw)),
                  pl.BlockSpec((1, tq, kvw), lambda i, s: (i, s, k_blk)),
                  pl.BlockSpec((1, tq, kvw), lambda i, s: (i, s, k_blk + 1)),
                  pl.BlockSpec((1, WINDOW, kvw), lambda i, s: (i, 0, 0)),
                  pl.BlockSpec((1, WINDOW, kvw), lambda i, s: (i, 0, 0)),
                  pl.BlockSpec((tq, LANES), lambda i, s: (s, 0)),
                  pl.BlockSpec((tq, LANES), lambda i, s: (s, 0)),
                  pl.BlockSpec((1, HEAD_BLOCK), lambda i, s: (0, 0)),
                  pl.BlockSpec((1, HEAD_BLOCK), lambda i, s: (0, 0)),
                  pl.BlockSpec((HEAD_BLOCK, HEAD_BLOCK), lambda i, s: (0, 0)),
                  pl.BlockSpec((HEAD_BLOCK, HEAD_BLOCK), lambda i, s: (0, 0))],
        out_specs=[pl.BlockSpec((1, tq, aw), lambda i, s: (i, s, 0)),
                   pl.BlockSpec((1, WINDOW, kvw), lambda i, s: (i, 0, 0)),
                   pl.BlockSpec((1, WINDOW, kvw), lambda i, s: (i, 0, 0))],
        scratch_shapes=[pltpu.VMEM((WINDOW + tq, kvw), F32),
                        pltpu.VMEM((WINDOW + tq, kvw), F32),
                        pltpu.VMEM((WINDOW + tq + CHUNK, 4 * kvw), BF16),
                        pltpu.VMEM((WINDOW + tq, 4 * kvw), BF16),
                        pltpu.VMEM((tq, aw), BF16),
                        pltpu.VMEM((n_blocks, 2 * CHUNK, WINDOW + 2 * CHUNK), F32),
                        pltpu.VMEM((n_blocks, 2 * CHUNK, WINDOW + 2 * CHUNK), BF16)],
        out_shape=[jax.ShapeDtypeStruct((bsz, t, aw), BF16),
                   jax.ShapeDtypeStruct((bsz, WINDOW, kvw), F32),
                   jax.ShapeDtypeStruct((bsz, WINDOW, kvw), F32)],
        compiler_params=_params(2),
        name="attention",
    )(sinks, z, z, z, k_cache, v_cache, cos, sin, q_g, k_g, mean_w, swap_w)


MERGE_SUB = 256


def _merge_kernel(a_ref, o_ref, gc_ref, ga_ref, wc_ref, wa_ref, wo_ref, x_ref, g1_ref, y_ref, mg):
    nb, tt, d = x_ref.shape
    tn = gc_ref.shape[2]
    n = pl.program_id(2)
    a = a_ref[...].reshape(nb * tt, -1)
    o = o_ref[...].reshape(nb * tt, -1)
    for c0 in range(0, tn, MERGE_SUB):
        cs = slice(c0, c0 + MERGE_SUB)
        ws = pl.ds(pl.multiple_of(n * tn + c0, MERGE_SUB), MERGE_SUB)
        yc = jnp.dot(a, wc_ref[:, ws], preferred_element_type=F32)
        ya = jnp.dot(o, wa_ref[:, ws], preferred_element_type=F32)
        gc = gc_ref[:, :, cs].astype(F32).reshape(nb * tt, MERGE_SUB)
        ga = ga_ref[:, :, cs].astype(F32).reshape(nb * tt, MERGE_SUB)
        merged = _sigmoid(gc) * yc + _sigmoid(ga) * ya
        mg[:, ws] = merged.astype(BF16)

    @pl.when(n == pl.num_programs(2) - 1)
    def _():
        proj = jnp.dot(mg[...], wo_ref[...], preferred_element_type=F32).reshape(nb, tt, d)
        y_ref[...] = x_ref[...] + g1_ref[...] * proj


def _merge(a, o, z, gc_col, ga_col, wc, wa, wo, x, mod3, row0, nb, tt, tn):
    bsz, t, d = x.shape
    c = a.shape[2]
    aw = o.shape[2]
    assert gc_col % tn == 0 and ga_col % tn == 0 and d % tn == 0
    return pl.pallas_call(
        _merge_kernel,
        grid=(bsz // nb, t // tt, d // tn),
        in_specs=[pl.BlockSpec((nb, tt, c), lambda i, s, n: (i, s, 0)),
                  pl.BlockSpec((nb, tt, aw), lambda i, s, n: (i, s, 0)),
                  pl.BlockSpec((nb, tt, tn), lambda i, s, n: (i, s, gc_col // tn + n)),
                  pl.BlockSpec((nb, tt, tn), lambda i, s, n: (i, s, ga_col // tn + n)),
                  pl.BlockSpec((c, d), lambda i, s, n: (0, 0), pipeline_mode=pl.Buffered(1)),
                  pl.BlockSpec((aw, d), lambda i, s, n: (0, 0), pipeline_mode=pl.Buffered(1)),
                  pl.BlockSpec((d, d), lambda i, s, n: (0, 0), pipeline_mode=pl.Buffered(1)),
                  pl.BlockSpec((nb, tt, d), lambda i, s, n: (i, s, 0)),
                  pl.BlockSpec((nb, 1, d), lambda i, s, n: (row0 // nb + i, 0, 2))],
        out_specs=pl.BlockSpec((nb, tt, d), lambda i, s, n: (i, s, 0)),
        out_shape=jax.ShapeDtypeStruct((bsz, t, d), F32),
        scratch_shapes=[pltpu.VMEM((nb * tt, d), BF16)],
        compiler_params=_params(3),
        name="merge_out",
    )(a, o, z, z, wc, wa, wo, x, mod3)


FFN_PAD = SUBLANES
FFN_PARTS = 2


def _ffn_kernel(x_ref, sc_ref, sh_ref, g2_ref, ng_ref, wg_ref, wv_ref, cw_ref, cb_ref, wd_ref, ctx_ref,
                y_ref, nf_ref, h_ref, ubuf, halo):
    nb, tt, d = x_ref.shape
    tf = wg_ref.shape[1]
    kw = cw_ref.shape[0]
    t = pl.program_id(1)
    f = pl.program_id(2)

    @pl.when(t == 0)
    def _():
        ubuf[:, 0:FFN_PAD, :] = ctx_ref[...]

    @pl.when(t > 0)
    def _():
        ubuf[:, 0:FFN_PAD, :] = halo[f]

    def step(first):
        cbias = cb_ref[...].reshape(1, 1, tf)
        taps = [cw_ref[k:k + 1, :].reshape(1, 1, tf) for k in range(kw)]
        for bs, ts in _row_parts(nb, tt, FFN_PARTS):
            nbp, ttp = bs.stop - bs.start, ts.stop - ts.start
            r0 = bs.start * tt + ts.start
            if first:
                hp = _adaln(x_ref[bs, ts, :], ng_ref[...], sc_ref[bs], sh_ref[bs])
                hp = hp.reshape(nbp * ttp, d).astype(BF16)
                h_ref[r0:r0 + nbp * ttp, :] = hp
            else:
                hp = h_ref[r0:r0 + nbp * ttp, :]
            ug = jnp.dot(hp, wg_ref[...], preferred_element_type=F32)
            uv = jnp.dot(hp, wv_ref[...], preferred_element_type=F32)
            ubuf[bs, FFN_PAD + ts.start:FFN_PAD + ts.stop, :] = ug.reshape(nbp, ttp, tf)
            conv = jnp.broadcast_to(cbias, (nbp, ttp, tf))
            for k in range(kw):
                lead = FFN_PAD - (kw - 1) + k + ts.start
                conv = conv + ubuf[bs, lead:lead + ttp, :] * taps[k]
            act = (conv * _sigmoid(conv)).reshape(nbp * ttp, tf) * uv
            part = jnp.dot(act.astype(BF16), wd_ref[...], preferred_element_type=F32).reshape(nbp, ttp, d)
            if first:
                y_ref[bs, ts, :] = part
            else:
                y_ref[bs, ts, :] += part
        tail = ubuf[:, tt:tt + FFN_PAD, :]
        halo[f] = tail
        nf_ref[:, :, pl.ds(pl.multiple_of(f * tf, tf), tf)] = tail

    pl.when(f == 0)(functools.partial(step, True))
    pl.when(f > 0)(functools.partial(step, False))

    @pl.when(f == pl.num_programs(2) - 1)
    def _():
        y_ref[...] = x_ref[...] + g2_ref[...] * y_ref[...]


def _ffn(x, mod3, row0, norm_g, w_up, conv_w, conv_b, w_down, ctx_pad, nb, tt, tf):
    bsz, t, d = x.shape
    dff = w_down.shape[0]
    kw = conv_w.shape[0]
    nf = dff // tf
    assert dff % tf == 0
    return pl.pallas_call(
        _ffn_kernel,
        grid=(bsz // nb, t // tt, nf),
        in_specs=[pl.BlockSpec((nb, tt, d), lambda i, s, f: (i, s, 0)),
                  pl.BlockSpec((nb, 1, d), lambda i, s, f: (row0 // nb + i, 0, 4)),
                  pl.BlockSpec((nb, 1, d), lambda i, s, f: (row0 // nb + i, 0, 3)),
                  pl.BlockSpec((nb, 1, d), lambda i, s, f: (row0 // nb + i, 0, 5)),
                  pl.BlockSpec((1, d), lambda i, s, f: (0, 0)),
                  pl.BlockSpec((d, tf), lambda i, s, f: (0, f)),
                  pl.BlockSpec((d, tf), lambda i, s, f: (0, nf + f)),
                  pl.BlockSpec((kw, tf), lambda i, s, f: (0, f)),
                  pl.BlockSpec((1, tf), lambda i, s, f: (0, f)),
                  pl.BlockSpec((tf, d), lambda i, s, f: (f, 0)),
                  pl.BlockSpec((nb, FFN_PAD, tf), lambda i, s, f: (i, 0, f))],
        out_specs=[pl.BlockSpec((nb, tt, d), lambda i, s, f: (i, s, 0)),
                   pl.BlockSpec((nb, FFN_PAD, dff), lambda i, s, f: (i, 0, 0))],
        out_shape=[jax.ShapeDtypeStruct((bsz, t, d), F32),
                   jax.ShapeDtypeStruct((bsz, FFN_PAD, dff), F32)],
        scratch_shapes=[pltpu.VMEM((nb * tt, d), BF16),
                        pltpu.VMEM((nb, FFN_PAD + tt, tf), F32),
                        pltpu.VMEM((nf, nb, FFN_PAD, tf), F32)],
        compiler_params=_params(3),
        name="conv_ffn",
    )(x, mod3, mod3, mod3, norm_g, w_up, w_up, conv_w, conv_b, w_down, ctx_pad)


def _rope_tables(pos):
    half = HEAD_DIM // 2
    inv_freq = 1.0 / (ROPE_THETA ** (jnp.arange(half, dtype=F32) / half))
    ang = pos.astype(F32)[:, None] * inv_freq[None, :]
    cos = jnp.cos(ang)
    sin = jnp.sin(ang)
    reps = LANES // HEAD_DIM
    return (jnp.tile(jnp.concatenate([cos, cos], axis=-1), (1, reps)),
            jnp.tile(jnp.concatenate([-sin, sin], axis=-1), (1, reps)))


def _front_pad(ctx, rows):
    return jnp.pad(ctx, ((0, 0), (rows - ctx.shape[1], 0), (0, 0)))


def _layer(x, mod3, row0, pos, conv_ctx, k_cache, v_cache, ffn_ctx, mask_prefix, p, tiles):
    bsz, t, d = x.shape
    c = p["conv_w"].shape[1]
    kvw = k_cache.shape[2] * k_cache.shape[3]
    aw = p["attn_o_w"].shape[0]
    nb, tt, tq = tiles["nb"], tiles["tt"], tiles["tq"]
    assert row0 % nb == 0 and bsz % nb == 0 and t % tt == 0 and t % tq == 0
    assert row0 % tiles["nb_merge"] == 0 and bsz % tiles["nb_merge"] == 0

    z = _in_proj(x, mod3, row0, p["norm1_g"], p["w_in"], p["b_in"], nb, tiles["tt_big"], tiles["tn_in"])
    a, nc = _conv_branch(z, _front_pad(conv_ctx, CONV_PAD), p["conv_w"], p["conv_b"], p["ln_g"], p["ln_b"],
                         nb, tiles["tt_big"])
    cos, sin = _rope_tables(pos)
    reps = HEAD_BLOCK // HEAD_DIM
    o, nk, nv = _attention(z, k_cache.reshape(bsz, -1, kvw), v_cache.reshape(bsz, -1, kvw), cos, sin,
                           jnp.tile(p["q_norm_g"], reps)[None, :], jnp.tile(p["k_norm_g"], reps)[None, :],
                           p["sinks"], 2 * c + 2 * d, 2 * c + 2 * d + aw, aw, kvw, tq, mask_prefix)
    gc_col = 2 * c
    x1 = _merge(a, o, z, gc_col, gc_col + d, p["conv_out_w"], p["attn_o_w"], p["w_out"], x, mod3, row0,
                tiles["nb_merge"], tt, tiles["tn_merge"])
    y, nf = _ffn(x1, mod3, row0, p["norm2_g"], p["ffn_up_w"], p["ffn_conv_w"], p["ffn_conv_b"],
                 p["ffn_down_w"], _front_pad(ffn_ctx, FFN_PAD), nb, tiles["tt_big"], tiles["tf"])
    kw = p["conv_w"].shape[0]
    fkw = p["ffn_conv_w"].shape[0]
    return (y, nc[:, CONV_PAD - (kw - 1):], nk.reshape(k_cache.shape), nv.reshape(v_cache.shape),
            nf[:, FFN_PAD - (fkw - 1):])


def kernel(x_prompt, x_sample, c_prompt, c_sample, cache_conv, cache_k, cache_v, cache_ffn_conv, mod_w, mod_b, norm1_g, w_in, b_in, conv_w, conv_b, ln_g, ln_b, conv_out_w, q_norm_g, k_norm_g, sinks, attn_o_w, w_out, norm2_g, ffn_up_w, ffn_conv_w, ffn_conv_b, ffn_down_w):
    depth = mod_w.shape[0]
    bp, tp, d = x_prompt.shape
    bs, ts, _ = x_sample.shape
    pos_p = jnp.arange(tp)
    pos_s = PAST_LEN + jnp.arange(ts)
    yp, ys = x_prompt, x_sample
    outs = [[] for _ in range(8)]
    tiles_p = dict(nb=1, nb_merge=1, tt=min(512, tp), tt_big=min(1024, tp), tq=min(1024, tp), tn_in=1536, tn_merge=2048,
                   tf=512)
    tiles_s = dict(nb=min(16, bs), nb_merge=min(8, bs), tt=ts, tt_big=ts, tq=ts, tn_in=1536, tn_merge=2048,
                   tf=512)
    for l in range(depth):
        glu_w = 2 * conv_w.shape[2]
        gates0 = glu_w + attn_o_w.shape[1] + 2 * cache_k.shape[3] * cache_k.shape[4]
        reorder = lambda m: jnp.concatenate([m[..., :glu_w], m[..., gates0:], m[..., glu_w:gates0]], axis=-1)
        p = dict(norm1_g=norm1_g[l][None, :], w_in=reorder(w_in[l]).astype(BF16), b_in=reorder(b_in[l])[None, :],
                 conv_w=conv_w[l], conv_b=conv_b[l][None, :], ln_g=ln_g[l][None, :], ln_b=ln_b[l][None, :],
                 conv_out_w=conv_out_w[l].astype(BF16), q_norm_g=q_norm_g[l], k_norm_g=k_norm_g[l],
                 sinks=sinks[l], attn_o_w=attn_o_w[l].astype(BF16), w_out=w_out[l].astype(BF16),
                 norm2_g=norm2_g[l][None, :], ffn_up_w=ffn_up_w[l].astype(BF16), ffn_conv_w=ffn_conv_w[l],
                 ffn_conv_b=ffn_conv_b[l][None, :], ffn_down_w=ffn_down_w[l].astype(BF16))
        c_all = jnp.concatenate([c_prompt, c_sample], axis=0)
        mod3 = _mod(c_all, mod_w[l], mod_b[l][None, :])[:, None, :]
        zeros_conv = jnp.zeros((bp,) + cache_conv.shape[2:], F32)
        zeros_kv = jnp.zeros((bp,) + cache_k.shape[2:], F32)
        zeros_ffn = jnp.zeros((bp,) + cache_ffn_conv.shape[2:], F32)
        yp, nc_p, nk_p, nv_p, nf_p = _layer(yp, mod3, 0, pos_p, zeros_conv, zeros_kv, zeros_kv, zeros_ffn,
                                            True, p, tiles_p)
        ys, nc_s, nk_s, nv_s, nf_s = _layer(ys, mod3, bp, pos_s, cache_conv[l], cache_k[l], cache_v[l],
                                            cache_ffn_conv[l], False, p, tiles_s)
        for lst, val in zip(outs, (nc_p, nc_s, nk_p, nk_s, nv_p, nv_s, nf_p, nf_s)):
            lst.append(val)
    return (yp, ys) + tuple(jnp.stack(o) for o in outs)
```
